```python
import jax, jax.numpy as jnp
from jax import lax
import numpy as np

D_MODEL = 1024
BATCH = 8
SEQ = 2048
DEPTH = 4

CHUNK = 64
N_MIXERS = 2
RWKV_HEAD = 64
RWKV_HEADS = D_MODEL // RWKV_HEAD
DECAY_LORA = 64
AAA_LORA = 64
MV_LORA = 32
GATE_LORA = 160
GN_EPS = 64e-5
CONV_WIDTH = 31
FFN_CONV_WIDTH = 3
D_FF = 2816
MEM_TOKENS = 256
XATTN_HEADS = 4
XATTN_HEAD_DIM = D_MODEL // XATTN_HEADS
NORM_EPS = 1e-6
LN_EPS = 1e-5

kernel_name = 'hybrid_rwkv7_conformer_memxattn_trunk'


def rmsnorm(x, g):
    x32 = x.astype(jnp.float32)
    y = x32 * lax.rsqrt(jnp.mean(x32 * x32, axis=-1, keepdims=True) + NORM_EPS)
    return (y * g).astype(x.dtype)


def layernorm(x, g, b):
    x32 = x.astype(jnp.float32)
    mu = jnp.mean(x32, axis=-1, keepdims=True)
    var = jnp.mean(jnp.square(x32 - mu), axis=-1, keepdims=True)
    return ((x32 - mu) * lax.rsqrt(var + LN_EPS) * g + b).astype(x.dtype)


def causal_dwconv(x, w):
    k_w = w.shape[0]
    return lax.conv_general_dilated(
        x, w[:, None, :].astype(x.dtype), window_strides=(1,), padding=[(k_w - 1, 0)],
        dimension_numbers=('NWC', 'WIO', 'NWC'), feature_group_count=x.shape[-1])


def rwkv7_step(S, inp):
    r_t, dec_t, k_t, v_t, kk_t, a_t = inp
    sa = jnp.einsum('bhij,bhj->bhi', S, -kk_t)
    S = (S * dec_t[:, :, None, :]
         + sa[..., None] * (kk_t * a_t)[:, :, None, :]
         + v_t[..., None] * k_t[:, :, None, :])
    y = jnp.einsum('bhij,bhj->bhi', S, r_t)
    return S, y


def rwkv7_time_mix(h, mu, w_r, w_k, w_v, w_o, w0, w1, w2, a0, a1, a2, g1, g2,
                   k_k, k_a, r_k, ln_g, ln_b, v_first, v_res):
    B, T, D = h.shape
    H, N = RWKV_HEADS, RWKV_HEAD
    xx = jnp.pad(h, ((0, 0), (1, 0), (0, 0)))[:, :T] - h
    xr, xw, xk, xv, xa, xg = (h + xx * mu[i] for i in range(6))
    r = xr @ w_r
    k = xk @ w_k
    v = xv @ w_v
    logw = -jax.nn.softplus(-(w0 + jnp.tanh(xw @ w1) @ w2)) - 0.5
    decay = jnp.exp(-jnp.exp(logw.astype(jnp.float32)))
    a = jax.nn.sigmoid(a0 + (xa @ a1) @ a2)
    g = jax.nn.sigmoid(xg @ g1) @ g2
    kk = (k * k_k).astype(jnp.float32).reshape(B, T, H, N)
    kk = kk / jnp.maximum(jnp.sqrt(jnp.sum(kk * kk, axis=-1, keepdims=True)), 1e-12)
    k = k * (1.0 + (a - 1.0) * k_a)
    if v_res is not None:
        v0, v1, v2 = v_res
        v = v + (v_first - v) * jax.nn.sigmoid(v0 + (xv @ v1) @ v2)

    def heads_t(t):
        return t.astype(jnp.float32).reshape(B, T, H, N).transpose(1, 0, 2, 3)

    seq = (heads_t(r), heads_t(decay), heads_t(k), heads_t(v),
           kk.transpose(1, 0, 2, 3), heads_t(a))
    S0 = jnp.zeros((B, H, N, N), jnp.float32)
    _, y = lax.scan(rwkv7_step, S0, seq)
    y = y.transpose(1, 0, 2, 3)
    m = jnp.mean(y, axis=-1, keepdims=True)
    var = jnp.mean(jnp.square(y - m), axis=-1, keepdims=True)
    y = ((y - m) * lax.rsqrt(var + GN_EPS)).reshape(B, T, D) * ln_g + ln_b
    rh = r.astype(jnp.float32).reshape(B, T, H, N)
    kh = k.astype(jnp.float32).reshape(B, T, H, N)
    vh = v.astype(jnp.float32).reshape(B, T, H, N)
    bonus = (jnp.sum(rh * kh * r_k, axis=-1, keepdims=True) * vh).reshape(B, T, D)
    out = ((y + bonus) * g).astype(h.dtype) @ w_o
    return out, v


def conformer_conv(h, w_in, b_in, dw, dw_b, ln_g, ln_b, w_out, b_out):
    D = h.shape[-1]
    u = h @ w_in + b_in
    u = u[..., :D] * jax.nn.sigmoid(u[..., D:])
    u = causal_dwconv(u, dw) + dw_b
    u = jax.nn.silu(layernorm(u, ln_g, ln_b))
    return u @ w_out + b_out


def memory_cross_attention(h, memn, w_q, w_kv, w_o):
    B, T, D = h.shape
    q = (h @ w_q).reshape(B, T, XATTN_HEADS, XATTN_HEAD_DIM)
    kv = memn @ w_kv
    km = kv[..., :D].reshape(B, -1, XATTN_HEADS, XATTN_HEAD_DIM)
    vm = kv[..., D:].reshape(B, -1, XATTN_HEADS, XATTN_HEAD_DIM)
    s = jnp.einsum('bthd,bmhd->bhtm', q, km).astype(jnp.float32) * (XATTN_HEAD_DIM ** -0.5)
    p = jax.nn.softmax(s, axis=-1).astype(h.dtype)
    o = jnp.einsum('bhtm,bmhd->bthd', p, vm).reshape(B, T, D)
    return o @ w_o


def conv_ffn(h, w_in, dw, w_out):
    u = causal_dwconv(h @ w_in, dw)
    gate, val = u[..., :D_FF], u[..., D_FF:]
    return (jax.nn.silu(gate) * val) @ w_out


def setup_inputs(seed: int = 0) -> dict:
    key = jax.random.key(seed)
    ks = iter(jax.random.split(key, 64))
    D = D_MODEL
    NA = (DEPTH + 1) // 2
    NB = DEPTH // 2
    NV = max(NA - 1, 0)

    def nrm(shape, scale):
        return jax.random.normal(next(ks), shape, jnp.float32) * scale

    def unif(shape, lo, hi):
        return jax.random.uniform(next(ks), shape, jnp.float32, lo, hi)

    def gain(shape):
        return 1.0 + nrm(shape, 0.05)

    sd = D ** -0.5
    return {
        'x': nrm((BATCH, SEQ, D), 1.0),
        'mem': nrm((BATCH, MEM_TOKENS, D), 1.0),
        'mem_norm_g': gain((D,)),
        'norm_mix_g': gain((DEPTH, D)),
        'norm_xattn_g': gain((DEPTH, D)),
        'norm_ffn_g': gain((DEPTH, D)),
        'final_norm_g': gain((D,)),
        'rwkv_mu': unif((NA, 6, D), 0.0, 1.0),
        'rwkv_w_r': nrm((NA, D, D), sd),
        'rwkv_w_k': nrm((NA, D, D), sd),
        'rwkv_w_v': nrm((NA, D, D), sd),
        'rwkv_w_o': nrm((NA, D, D), sd),
        'rwkv_w0': unif((NA, D), -6.0, 1.0),
        'rwkv_w1': nrm((NA, D, DECAY_LORA), sd),
        'rwkv_w2': nrm((NA, DECAY_LORA, D), 0.5 * DECAY_LORA ** -0.5),
        'rwkv_a0': nrm((NA, D), 0.3),
        'rwkv_a1': nrm((NA, D, AAA_LORA), sd),
        'rwkv_a2': nrm((NA, AAA_LORA, D), 0.5 * AAA_LORA ** -0.5),
        'rwkv_g1': nrm((NA, D, GATE_LORA), sd),
        'rwkv_g2': nrm((NA, GATE_LORA, D), GATE_LORA ** -0.5),
        'rwkv_k_k': 0.85 + nrm((NA, D), 0.05),
        'rwkv_k_a': 1.0 + nrm((NA, D), 0.05),
        'rwkv_r_k': nrm((NA, RWKV_HEADS, RWKV_HEAD), 0.1),
        'rwkv_ln_g': gain((NA, D)),
        'rwkv_ln_b': nrm((NA, D), 0.02),
        'rwkv_v0': nrm((NV, D), 0.3),
        'rwkv_v1': nrm((NV, D, MV_LORA), sd),
        'rwkv_v2': nrm((NV, MV_LORA, D), 0.5 * MV_LORA ** -0.5),
        'conv_w_in': nrm((NB, D, 2 * D), sd),
        'conv_b_in': nrm((NB, 2 * D), 0.02),
        'conv_dw': nrm((NB, CONV_WIDTH, D), CONV_WIDTH ** -0.5),
        'conv_dw_b': nrm((NB, D), 0.02),
        'conv_ln_g': gain((NB, D)),
        'conv_ln_b': nrm((NB, D), 0.02),
        'conv_w_out': nrm((NB, D, D), sd),
        'conv_b_out': nrm((NB, D), 0.02),
        'xattn_w_q': nrm((DEPTH, D, D), sd),
        'xattn_w_kv': nrm((DEPTH, D, 2 * D), sd),
        'xattn_w_o': nrm((DEPTH, D, D), sd),
        'ffn_w_in': nrm((DEPTH, D, 2 * D_FF), sd),
        'ffn_dw': nrm((DEPTH, FFN_CONV_WIDTH, 2 * D_FF), FFN_CONV_WIDTH ** -0.5),
        'ffn_w_out': nrm((DEPTH, D_FF, D), D_FF ** -0.5),
    }


def reference(x, mem, mem_norm_g, norm_mix_g, norm_xattn_g, norm_ffn_g, final_norm_g,
              rwkv_mu, rwkv_w_r, rwkv_w_k, rwkv_w_v, rwkv_w_o, rwkv_w0, rwkv_w1, rwkv_w2,
              rwkv_a0, rwkv_a1, rwkv_a2, rwkv_g1, rwkv_g2, rwkv_k_k, rwkv_k_a, rwkv_r_k,
              rwkv_ln_g, rwkv_ln_b, rwkv_v0, rwkv_v1, rwkv_v2,
              conv_w_in, conv_b_in, conv_dw, conv_dw_b, conv_ln_g, conv_ln_b,
              conv_w_out, conv_b_out, xattn_w_q, xattn_w_kv, xattn_w_o,
              ffn_w_in, ffn_dw, ffn_w_out):
    memn = rmsnorm(mem, mem_norm_g)
    v_first = None
    ia = 0
    ib = 0
    for layer in range(DEPTH):
        h = rmsnorm(x, norm_mix_g[layer])
        if layer % N_MIXERS == 0:
            v_res = None if ia == 0 else (rwkv_v0[ia - 1], rwkv_v1[ia - 1], rwkv_v2[ia - 1])
            out, v = rwkv7_time_mix(
                h, rwkv_mu[ia], rwkv_w_r[ia], rwkv_w_k[ia], rwkv_w_v[ia], rwkv_w_o[ia],
                rwkv_w0[ia], rwkv_w1[ia], rwkv_w2[ia], rwkv_a0[ia], rwkv_a1[ia], rwkv_a2[ia],
                rwkv_g1[ia], rwkv_g2[ia], rwkv_k_k[ia], rwkv_k_a[ia], rwkv_r_k[ia],
                rwkv_ln_g[ia], rwkv_ln_b[ia], v_first, v_res)
            if ia == 0:
                v_first = v
            ia += 1
        else:
            out = conformer_conv(h, conv_w_in[ib], conv_b_in[ib], conv_dw[ib], conv_dw_b[ib],
                                 conv_ln_g[ib], conv_ln_b[ib], conv_w_out[ib], conv_b_out[ib])
            ib += 1
        x = x + out
        x = x + memory_cross_attention(rmsnorm(x, norm_xattn_g[layer]), memn,
                                       xattn_w_q[layer], xattn_w_kv[layer], xattn_w_o[layer])
        x = x + conv_ffn(rmsnorm(x, norm_ffn_g[layer]), ffn_w_in[layer], ffn_dw[layer],
                         ffn_w_out[layer])
    return rmsnorm(x, final_norm_g)
```

```python
import functools

import jax
import jax.numpy as jnp
from jax import lax
from jax.experimental import pallas as pl
from jax.experimental.pallas import tpu as pltpu

F32 = jnp.float32
BF16 = jnp.bfloat16

RWKV_HEAD = 64
GN_EPS = 64e-5
NORM_EPS = 1e-6
LN_EPS = 1e-5
XATTN_HEADS = 4
SUBLANES = 8
SEG_BLOCK = 256
VMEM_LIMIT = 56 * 1024 * 1024


def _params(n_grid):
    return pltpu.CompilerParams(
        dimension_semantics=("arbitrary",) * n_grid, vmem_limit_bytes=VMEM_LIMIT)


def _dot(a, b):
    return jnp.dot(a, b, preferred_element_type=F32)


def _rms(x, g):
    return x * lax.rsqrt(jnp.mean(x * x, axis=-1, keepdims=True) + NORM_EPS) * g


def _sigmoid(x):
    return 1.0 / (1.0 + jnp.exp(-x))


def _softplus(x):
    return jnp.maximum(x, 0.0) + jnp.log(1.0 + jnp.exp(-jnp.abs(x)))


def _seg_sum(y, e):
    outs = []
    for c in range(y.shape[-1] // SEG_BLOCK):
        yb = y[:, c * SEG_BLOCK:(c + 1) * SEG_BLOCK]
        hi = yb.astype(BF16)
        lo = (yb - hi.astype(F32)).astype(BF16)
        outs.append(_dot(hi, e) + _dot(lo, e))
    return jnp.concatenate(outs, axis=-1)


def _seg_matrix():
    i = lax.broadcasted_iota(jnp.int32, (SEG_BLOCK, SEG_BLOCK), 0) // RWKV_HEAD
    j = lax.broadcasted_iota(jnp.int32, (SEG_BLOCK, SEG_BLOCK), 1) // RWKV_HEAD
    return (i == j).astype(BF16)


def _const_spec(shape):
    zeros = (0,) * len(shape)
    return pl.BlockSpec(shape, lambda *_: zeros)


def _tile_spec(tm, d):
    return pl.BlockSpec((1, tm, d), lambda b, t: (b, t, 0))


def _prev_rows_spec(tm, d):
    per = tm // SUBLANES
    return pl.BlockSpec((1, SUBLANES, d), lambda b, t: (b, jnp.maximum(t * per - 1, 0), 0))


def _rwkv_pre_kernel(has_vres, *refs):
    if has_vres:
        (x_ref, xp_ref, g_ref, mu_ref, wr_ref, wk_ref, wv_ref, w0_ref, w1_ref, w2_ref,
         a0_ref, a1_ref, a2_ref, g1_ref, g2_ref, kk_ref, ka_ref, e_ref,
         vf_ref, v0_ref, v1_ref, v2_ref,
         r_out, w_out, k_out, v_out, al_out, be_out, g_out) = refs
    else:
        (x_ref, xp_ref, g_ref, mu_ref, wr_ref, wk_ref, wv_ref, w0_ref, w1_ref, w2_ref,
         a0_ref, a1_ref, a2_ref, g1_ref, g2_ref, kk_ref, ka_ref, e_ref,
         r_out, w_out, k_out, v_out, al_out, be_out, g_out) = refs
    t = pl.program_id(1)
    g = g_ref[...]
    h = _rms(x_ref[0], g)
    hp = _rms(xp_ref[0], g)
    prev = jnp.where(t > 0, hp[SUBLANES - 1:SUBLANES, :], 0.0)
    row = lax.broadcasted_iota(jnp.int32, h.shape, 0)
    hs = jnp.where(row == 0, prev, pltpu.roll(h, 1, axis=0))
    xx = hs - h

    def mix(i):
        return (h + xx * mu_ref[i:i + 1, :]).astype(BF16)

    e = e_ref[...]
    r = _dot(mix(0), wr_ref[...])
    wl = _dot(jnp.tanh(_dot(mix(1), w1_ref[...])).astype(BF16), w2_ref[...])
    k = _dot(mix(2), wk_ref[...])
    xv = mix(3)
    v = _dot(xv, wv_ref[...])
    al = _dot(_dot(mix(4), a1_ref[...]).astype(BF16), a2_ref[...])
    gate = _dot(_sigmoid(_dot(mix(5), g1_ref[...])).astype(BF16), g2_ref[...])

    logw = -_softplus(-(w0_ref[...] + wl)) - 0.5
    decay = jnp.exp(-jnp.exp(logw))
    a = _sigmoid(a0_ref[...] + al)
    kk = k * kk_ref[...]
    kk = kk / jnp.maximum(jnp.sqrt(_seg_sum(kk * kk, e)), 1e-12)
    k = k * (1.0 + (a - 1.0) * ka_ref[...])
    if has_vres:
        vl = _dot(_dot(xv, v1_ref[...]).astype(BF16), v2_ref[...])
        v = v + (vf_ref[0] - v) * _sigmoid(v0_ref[...] + vl)
    r_out[0] = r
    w_out[0] = decay
    k_out[0] = k
    v_out[0] = v
    al_out[0] = -kk
    be_out[0] = kk * a
    g_out[0] = gate


def _rwkv_pre(x, g, mu, wr, wk, wv, w0, w1, w2, a0, a1, a2, g1, g2, k_k, k_a, v_first, v_res, tm):
    B, T, D = x.shape
    has_vres = v_res is not None
    row = lambda p: p.reshape(1, D)
    bf = lambda p: p.astype(BF16)
    args = [x, x, row(g), mu, bf(wr), bf(wk), bf(wv), row(w0), bf(w1), bf(w2),
            row(a0), bf(a1), bf(a2), bf(g1), bf(g2), row(k_k), row(k_a), _seg_matrix()]
    specs = [_tile_spec(tm, D), _prev_rows_spec(tm, D)] + [_const_spec(a.shape) for a in args[2:]]
    if has_vres:
        v0, v1, v2 = v_res
        extra = [v_first, row(v0), bf(v1), bf(v2)]
        args += extra
        specs += [_tile_spec(tm, D)] + [_const_spec(a.shape) for a in extra[1:]]
    out = jax.ShapeDtypeStruct((B, T, D), F32)
    return pl.pallas_call(
        functools.partial(_rwkv_pre_kernel, has_vres),
        grid=(B, T // tm),
        in_specs=specs,
        out_specs=[_tile_spec(tm, D)] * 7,
        out_shape=[out] * 7,
        compiler_params=_params(2),
        name="rwkv_pre",
    )(*args)


def _rwkv_rec_kernel(r_ref, w_ref, k_ref, v_ref, al_ref, be_ref, y_ref, s_ref):
    n = s_ref.shape[0]

    @pl.when(pl.program_id(0) == 0)
    def _():
        s_ref[...] = jnp.zeros_like(s_ref)

    def step(t, carry):
        v = v_ref[t]
        sa = s_ref[0] * al_ref[t, 0:1, :]
        for j in range(1, n):
            sa = sa + s_ref[j] * al_ref[t, j:j + 1, :]
        y = None
        for j in range(n):
            sj = (s_ref[j] * w_ref[t, j:j + 1, :] + sa * be_ref[t, j:j + 1, :]
                  + v * k_ref[t, j:j + 1, :])
            s_ref[j] = sj
            yj = sj * r_ref[t, j:j + 1, :]
            y = yj if y is None else y + yj
        y_ref[t] = y
        return carry

    lax.fori_loop(0, r_ref.shape[0], step, 0)


def _rwkv_rec(r, w, k, v, al, be, tb):
    T, N, L = r.shape
    spec = pl.BlockSpec((tb, N, L), lambda t: (t, 0, 0))
    return pl.pallas_call(
        _rwkv_rec_kernel,
        grid=(T // tb,),
        in_specs=[spec] * 6,
        out_specs=spec,
        out_shape=jax.ShapeDtypeStruct((T, N, L), F32),
        scratch_shapes=[pltpu.VMEM((N, N, L), F32)],
        compiler_params=_params(1),
        name="rwkv_rec",
    )(r, w, k, v, al, be)


def _to_lanes(x):
    B, T, D = x.shape
    H = D // RWKV_HEAD
    return x.reshape(B, T, H, RWKV_HEAD).transpose(1, 3, 0, 2).reshape(T, RWKV_HEAD, B * H)


def _from_lanes(y, B):
    T, N, L = y.shape
    H = L // B
    return y.reshape(T, N, B, H).transpose(2, 0, 3, 1).reshape(B, T, H * N)


def _rwkv_post_kernel(x_ref, y_ref, r_ref, k_ref, v_ref, g_ref, lng_ref, lnb_ref, rk_ref,
                      wo_ref, e_ref, o_ref):
    e = e_ref[...]
    y = y_ref[0]
    inv_n = 1.0 / RWKV_HEAD
    d = y - _seg_sum(y, e) * inv_n
    var = _seg_sum(d * d, e) * inv_n
    yn = d * lax.rsqrt(var + GN_EPS) * lng_ref[...] + lnb_ref[...]
    bonus = _seg_sum(r_ref[0] * k_ref[0] * rk_ref[...], e) * v_ref[0]
    out = ((yn + bonus) * g_ref[0]).astype(BF16)
    o_ref[0] = x_ref[0] + _dot(out, wo_ref[...])


def _rwkv_post(x, y, r, k, v, gate, ln_g, ln_b, r_k, w_o, tm):
    B, T, D = x.shape
    row = lambda p: p.reshape(1, D)
    consts = [row(ln_g), row(ln_b), row(r_k), w_o.astype(BF16), _seg_matrix()]
    return pl.pallas_call(
        _rwkv_post_kernel,
        grid=(B, T // tm),
        in_specs=[_tile_spec(tm, D)] * 6 + [_const_spec(c.shape) for c in consts],
        out_specs=_tile_spec(tm, D),
        out_shape=jax.ShapeDtypeStruct((B, T, D), F32),
        compiler_params=_params(2),
        name="rwkv_post",
    )(x, y, r, k, v, gate, *consts)


def _conformer_kernel(kw, tm, x_ref, g_ref, win_ref, bin_ref, dw_ref, dwb_ref, lng_ref, lnb_ref,
                      wout_ref, bout_ref, o_ref, cbuf, obuf):
    D = x_ref.shape[-1]
    hdr = cbuf.shape[0] - tm
    x = x_ref[0]
    h = _rms(x, g_ref[...]).astype(BF16)
    u = _dot(h, win_ref[...]) + bin_ref[...]
    glu = u[:, :D] * _sigmoid(u[:, D:])

    @pl.when(pl.program_id(1) == 0)
    def _():
        cbuf[0:hdr, :] = jnp.zeros((hdr, D), F32)

    cbuf[hdr:hdr + tm, :] = glu
    rb = 64
    cb = 128
    for c in range(D // cb):
        cs = slice(c * cb, (c + 1) * cb)
        for r0 in range(0, tm, rb):
            acc = None
            for k in range(kw):
                term = dw_ref[k:k + 1, cs] * cbuf[r0 + hdr - (kw - 1) + k:r0 + hdr - (kw - 1) + k + rb, cs]
                acc = term if acc is None else acc + term
            obuf[r0:r0 + rb, cs] = acc
    cbuf[0:hdr, :] = cbuf[tm:tm + hdr, :]

    c = obuf[...] + dwb_ref[...]
    mu = jnp.mean(c, axis=-1, keepdims=True)
    var = jnp.mean(jnp.square(c - mu), axis=-1, keepdims=True)
    z = (c - mu) * lax.rsqrt(var + LN_EPS) * lng_ref[...] + lnb_ref[...]
    z = (z * _sigmoid(z)).astype(BF16)
    o_ref[0] = x + (_dot(z, wout_ref[...]) + bout_ref[...])


def _conformer(x, g, w_in, b_in, dw, dw_b, ln_g, ln_b, w_out, b_out, tm):
    B, T, D = x.shape
    kw = dw.shape[0]
    hdr = -(-(kw - 1) // SUBLANES) * SUBLANES
    row = lambda p: p.reshape(1, -1)
    consts = [row(g), w_in.astype(BF16), row(b_in), dw, row(dw_b), row(ln_g), row(ln_b),
              w_out.astype(BF16), row(b_out)]
    return pl.pallas_call(
        functools.partial(_conformer_kernel, kw, tm),
        grid=(B, T // tm),
        in_specs=[_tile_spec(tm, D)] + [_const_spec(c.shape) for c in consts],
        out_specs=_tile_spec(tm, D),
        out_shape=jax.ShapeDtypeStruct((B, T, D), F32),
        scratch_shapes=[pltpu.VMEM((hdr + tm, D), F32), pltpu.VMEM((tm, D), F32)],
        compiler_params=_params(2),
        name="conformer",
    )(x, *consts)


def _kv_kernel(mem_ref, g_ref, w_ref, o_ref):
    memn = _rms(mem_ref[0], g_ref[...]).astype(BF16)
    o_ref[0, 0] = _dot(memn, w_ref[0]).astype(BF16)


def _kv_proj(mem, g, w_kv):
    B, M, D = mem.shape
    L = w_kv.shape[0]
    return pl.pallas_call(
        _kv_kernel,
        grid=(L, B),
        in_specs=[pl.BlockSpec((1, M, D), lambda l, b: (b, 0, 0)),
                  pl.BlockSpec((1, D), lambda l, b: (0, 0)),
                  pl.BlockSpec((1, D, 2 * D), lambda l, b: (l, 0, 0))],
        out_specs=pl.BlockSpec((1, 1, M, 2 * D), lambda l, b: (l, b, 0, 0)),
        out_shape=jax.ShapeDtypeStruct((L, B, M, 2 * D), BF16),
        compiler_params=_params(2),
        name="kv_proj",
    )(mem, g.reshape(1, D), w_kv.astype(BF16))


def _xattn_kernel(x_ref, g_ref, wq_ref, kv_ref, wo_ref, o_ref):
    D = x_ref.shape[-1]
    hd = D // XATTN_HEADS
    x = x_ref[0]
    h = _rms(x, g_ref[...]).astype(BF16)
    q = _dot(h, wq_ref[...]).astype(BF16)
    outs = []
    for i in range(XATTN_HEADS):
        km = kv_ref[0, 0, :, i * hd:(i + 1) * hd]
        vm = kv_ref[0, 0, :, D + i * hd:D + (i + 1) * hd]
        s = lax.dot_general(q[:, i * hd:(i + 1) * hd], km, (((1,), (1,)), ((), ())),
                            preferred_element_type=F32) * (hd ** -0.5)
        s = jnp.exp(s - jnp.max(s, axis=-1, keepdims=True))
        p = (s / jnp.sum(s, axis=-1, keepdims=True)).astype(BF16)
        outs.append(_dot(p, vm).astype(BF16))
    o = jnp.concatenate(outs, axis=-1)
    o_ref[0] = x + _dot(o, wo_ref[...])


def _xattn(x, g, w_q, kv, layer, w_o, tm):
    B, T, D = x.shape
    M = kv.shape[2]
    consts = [g.reshape(1, D), w_q.astype(BF16)]
    return pl.pallas_call(
        _xattn_kernel,
        grid=(B, T // tm),
        in_specs=[_tile_spec(tm, D)] + [_const_spec(c.shape) for c in consts]
        + [pl.BlockSpec((1, 1, M, 2 * D), lambda b, t: (layer, b, 0, 0)), _const_spec((D, D))],
        out_specs=_tile_spec(tm, D),
        out_shape=jax.ShapeDtypeStruct((B, T, D), F32),
        compiler_params=_params(2),
        name="xattn",
    )(x, *consts, kv, w_o.astype(BF16))


def _ffn_kernel(kw, cw, has_final, *refs):
    if has_final:
        x_ref, g_ref, win_ref, dw_ref, wout_ref, fg_ref, o_ref, ubuf, carry = refs
    else:
        x_ref, g_ref, win_ref, dw_ref, wout_ref, o_ref, ubuf, carry = refs
    tm = x_ref.shape[1]
    F = wout_ref.shape[0]
    hdr = SUBLANES
    x = x_ref[0]
    h = _rms(x, g_ref[...]).astype(BF16)

    @pl.when(pl.program_id(1) == 0)
    def _():
        carry[...] = jnp.zeros_like(carry)

    def conv(c0):
        cs = slice(c0, c0 + cw)
        ubuf[0:hdr, :] = carry[:, cs]
        u = _dot(h, win_ref[:, cs])
        ubuf[hdr:hdr + tm, :] = u
        carry[:, cs] = u[tm - hdr:, :]
        out = dw_ref[kw - 1:kw, cs] * u
        for k in range(kw - 1):
            off = hdr - (kw - 1) + k
            out = out + dw_ref[k:k + 1, cs] * ubuf[off:off + tm, :]
        return out

    acc = None
    for c0 in range(0, F, cw):
        gate = conv(c0)
        val = conv(F + c0)
        act = (gate * _sigmoid(gate) * val).astype(BF16)
        part = _dot(act, wout_ref[c0:c0 + cw, :])
        acc = part if acc is None else acc + part
    y = x + acc
    if has_final:
        y = _rms(y, fg_ref[...])
    o_ref[0] = y


def _ffn(x, g, w_in, dw, w_out, final_g, tm, cw):
    B, T, D = x.shape
    kw = dw.shape[0]
    F = w_out.shape[0]
    has_final = final_g is not None
    consts = [g.reshape(1, D), w_in.astype(BF16), dw, w_out.astype(BF16)]
    if has_final:
        consts.append(final_g.reshape(1, D))
    return pl.pallas_call(
        functools.partial(_ffn_kernel, kw, cw, has_final),
        grid=(B, T // tm),
        in_specs=[_tile_spec(tm, D)] + [_const_spec(c.shape) for c in consts],
        out_specs=_tile_spec(tm, D),
        out_shape=jax.ShapeDtypeStruct((B, T, D), F32),
        scratch_shapes=[pltpu.VMEM((SUBLANES + tm, cw), F32), pltpu.VMEM((SUBLANES, 2 * F), F32)],
        compiler_params=_params(2),
        name="ffn",
    )(x, *consts)


TM = 256
REC_TB = 64
FFN_CW = 256


def rwkv_layer(x, g, mu, w_r, w_k, w_v, w_o, w0, w1, w2, a0, a1, a2, g1, g2, k_k, k_a, r_k,
               ln_g, ln_b, v_first, v_res, tm, tb):
    r, w, k, v, al, be, gate = _rwkv_pre(x, g, mu, w_r, w_k, w_v, w0, w1, w2, a0, a1, a2, g1, g2,
                                         k_k, k_a, v_first, v_res, tm)
    y = _rwkv_rec(*(_to_lanes(a) for a in (r, w, k, v, al, be)), tb)
    x = _rwkv_post(x, _from_lanes(y, x.shape[0]), r, k, v, gate, ln_g, ln_b, r_k.reshape(-1), w_o, tm)
    return x, v


def kernel(x, mem, mem_norm_g, norm_mix_g, norm_xattn_g, norm_ffn_g, final_norm_g, rwkv_mu, rwkv_w_r, rwkv_w_k, rwkv_w_v, rwkv_w_o, rwkv_w0, rwkv_w1, rwkv_w2, rwkv_a0, rwkv_a1, rwkv_a2, rwkv_g1, rwkv_g2, rwkv_k_k, rwkv_k_a, rwkv_r_k, rwkv_ln_g, rwkv_ln_b, rwkv_v0, rwkv_v1, rwkv_v2, conv_w_in, conv_b_in, conv_dw, conv_dw_b, conv_ln_g, conv_ln_b, conv_w_out, conv_b_out, xattn_w_q, xattn_w_kv, xattn_w_o, ffn_w_in, ffn_dw, ffn_w_out):
    B, T, D = x.shape
    depth = norm_mix_g.shape[0]
    tm = min(TM, T)
    kv = _kv_proj(mem, mem_norm_g, xattn_w_kv)
    v_first = None
    ia = 0
    ib = 0
    for layer in range(depth):
        if layer % 2 == 0:
            v_res = None if ia == 0 else (rwkv_v0[ia - 1], rwkv_v1[ia - 1], rwkv_v2[ia - 1])
            x, v = rwkv_layer(
                x, norm_mix_g[layer], rwkv_mu[ia], rwkv_w_r[ia], rwkv_w_k[ia], rwkv_w_v[ia],
                rwkv_w_o[ia], rwkv_w0[ia], rwkv_w1[ia], rwkv_w2[ia], rwkv_a0[ia], rwkv_a1[ia],
                rwkv_a2[ia], rwkv_g1[ia], rwkv_g2[ia], rwkv_k_k[ia], rwkv_k_a[ia], rwkv_r_k[ia],
                rwkv_ln_g[ia], rwkv_ln_b[ia], v_first, v_res, tm, min(REC_TB, T))
            if ia == 0:
                v_first = v
            ia += 1
        else:
            x = _conformer(x, norm_mix_g[layer], conv_w_in[ib], conv_b_in[ib], conv_dw[ib],
                           conv_dw_b[ib], conv_ln_g[ib], conv_ln_b[ib], conv_w_out[ib],
                           conv_b_out[ib], tm)
            ib += 1
        x = _xattn(x, norm_xattn_g[layer], xattn_w_q[layer], kv, layer, xattn_w_o[layer], tm)
        x = _ffn(x, norm_ffn_g[layer], ffn_w_in[layer], ffn_dw[layer], ffn_w_out[layer],
                 final_norm_g if layer == depth - 1 else None, tm, FFN_CW)
    return x
```

```python
import functools

import jax
import jax.numpy as jnp
from jax import lax
from jax.experimental import pallas as pl
from jax.experimental.pallas import tpu as pltpu

F32 = jnp.float32
BF16 = jnp.bfloat16

RWKV_HEAD = 64
GN_EPS = 64e-5
NORM_EPS = 1e-6
LN_EPS = 1e-5
XATTN_HEADS = 4
SUBLANES = 8
SEG_BLOCK = 256
VMEM_LIMIT = 56 * 1024 * 1024


def _params(n_grid):
    return pltpu.CompilerParams(
        dimension_semantics=("arbitrary",) * n_grid, vmem_limit_bytes=VMEM_LIMIT)


def _dot(a, b):
    return jnp.dot(a, b, preferred_element_type=F32)


def _rms(x, g):
    return x * lax.rsqrt(jnp.mean(x * x, axis=-1, keepdims=True) + NORM_EPS) * g


def _sigmoid(x):
    return 1.0 / (1.0 + jnp.exp(-x))


def _softplus(x):
    return jnp.maximum(x, 0.0) + jnp.log(1.0 + jnp.exp(-jnp.abs(x)))


def _seg_sum(y, e):
    outs = []
    for c in range(y.shape[-1] // SEG_BLOCK):
        yb = y[:, c * SEG_BLOCK:(c + 1) * SEG_BLOCK]
        hi = yb.astype(BF16)
        lo = (yb - hi.astype(F32)).astype(BF16)
        outs.append(_dot(hi, e) + _dot(lo, e))
    return jnp.concatenate(outs, axis=-1)


def _seg_matrix():
    i = lax.broadcasted_iota(jnp.int32, (SEG_BLOCK, SEG_BLOCK), 0) // RWKV_HEAD
    j = lax.broadcasted_iota(jnp.int32, (SEG_BLOCK, SEG_BLOCK), 1) // RWKV_HEAD
    return (i == j).astype(BF16)


def _const_spec(shape):
    zeros = (0,) * len(shape)
    return pl.BlockSpec(shape, lambda *_: zeros, pipeline_mode=pl.Buffered(1))


def _tile_spec(tm, d):
    return pl.BlockSpec((1, tm, d), lambda b, t: (b, t, 0))


def _prev_rows_spec(tm, d):
    per = tm // SUBLANES
    return pl.BlockSpec((1, SUBLANES, d), lambda b, t: (b, jnp.maximum(t * per - 1, 0), 0))


def _rwkv_pre_kernel(has_vres, *refs):
    if has_vres:
        (x_ref, xp_ref, g_ref, mu_ref, wr_ref, wk_ref, wv_ref, w0_ref, w1_ref, w2_ref,
         a0_ref, a1_ref, a2_ref, g1_ref, g2_ref, kk_ref, ka_ref, e_ref,
         vf_ref, v0_ref, v1_ref, v2_ref,
         r_out, w_out, k_out, v_out, al_out, be_out, g_out) = refs
    else:
        (x_ref, xp_ref, g_ref, mu_ref, wr_ref, wk_ref, wv_ref, w0_ref, w1_ref, w2_ref,
         a0_ref, a1_ref, a2_ref, g1_ref, g2_ref, kk_ref, ka_ref, e_ref,
         r_out, w_out, k_out, v_out, al_out, be_out, g_out) = refs
    t = pl.program_id(1)
    g = g_ref[...]
    h = _rms(x_ref[0], g)
    hp = _rms(xp_ref[0], g)
    prev = jnp.where(t > 0, hp[SUBLANES - 1:SUBLANES, :], 0.0)
    row = lax.broadcasted_iota(jnp.int32, h.shape, 0)
    hs = jnp.where(row == 0, prev, pltpu.roll(h, 1, axis=0))
    xx = hs - h

    def mix(i):
        return (h + xx * mu_ref[i:i + 1, :]).astype(BF16)

    e = e_ref[...]
    r = _dot(mix(0), wr_ref[...])
    wl = _dot(jnp.tanh(_dot(mix(1), w1_ref[...])).astype(BF16), w2_ref[...])
    k = _dot(mix(2), wk_ref[...])
    xv = mix(3)
    v = _dot(xv, wv_ref[...])
    al = _dot(_dot(mix(4), a1_ref[...]).astype(BF16), a2_ref[...])
    gate = _dot(_sigmoid(_dot(mix(5), g1_ref[...])).astype(BF16), g2_ref[...])

    logw = -_softplus(-(w0_ref[...] + wl)) - 0.5
    decay = jnp.exp(-jnp.exp(logw))
    a = _sigmoid(a0_ref[...] + al)
    kk = k * kk_ref[...]
    kk = kk / jnp.maximum(jnp.sqrt(_seg_sum(kk * kk, e)), 1e-12)
    k = k * (1.0 + (a - 1.0) * ka_ref[...])
    if has_vres:
        vl = _dot(_dot(xv, v1_ref[...]).astype(BF16), v2_ref[...])
        v = v + (vf_ref[0] - v) * _sigmoid(v0_ref[...] + vl)
    r_out[0] = r
    w_out[0] = decay
    k_out[0] = k
    v_out[0] = v
    al_out[0] = -kk
    be_out[0] = kk * a
    g_out[0] = gate


def _rwkv_pre(x, g, mu, wr, wk, wv, w0, w1, w2, a0, a1, a2, g1, g2, k_k, k_a, v_first, v_res, tm):
    B, T, D = x.shape
    has_vres = v_res is not None
    row = lambda p: p.reshape(1, D)
    bf = lambda p: p.astype(BF16)
    args = [x, x, row(g), mu, bf(wr), bf(wk), bf(wv), row(w0), bf(w1), bf(w2),
            row(a0), bf(a1), bf(a2), bf(g1), bf(g2), row(k_k), row(k_a), _seg_matrix()]
    specs = [_tile_spec(tm, D), _prev_rows_spec(tm, D)] + [_const_spec(a.shape) for a in args[2:]]
    if has_vres:
        v0, v1, v2 = v_res
        extra = [v_first, row(v0), bf(v1), bf(v2)]
        args += extra
        specs += [_tile_spec(tm, D)] + [_const_spec(a.shape) for a in extra[1:]]
    out = jax.ShapeDtypeStruct((B, T, D), F32)
    return pl.pallas_call(
        functools.partial(_rwkv_pre_kernel, has_vres),
        grid=(B, T // tm),
        in_specs=specs,
        out_specs=[_tile_spec(tm, D)] * 7,
        out_shape=[out] * 7,
        compiler_params=_params(2),
        name="rwkv_pre",
    )(*args)


def _rwkv_rec_kernel(r_ref, w_ref, k_ref, v_ref, al_ref, be_ref, y_ref, s_ref):
    nq, jb = s_ref.shape[0], s_ref.shape[1]

    @pl.when(pl.program_id(0) == 0)
    def _():
        s_ref[...] = jnp.zeros_like(s_ref)

    def step(t, carry):
        v = v_ref[t]

        def sa_block(q, sa):
            for jj in range(jb):
                sa = sa + s_ref[q, jj] * al_ref[t, q, jj:jj + 1, :]
            return sa

        sa = lax.fori_loop(0, nq, sa_block, jnp.zeros_like(v))

        def update_block(q, y):
            for jj in range(jb):
                sj = (s_ref[q, jj] * w_ref[t, q, jj:jj + 1, :] + sa * be_ref[t, q, jj:jj + 1, :]
                      + v * k_ref[t, q, jj:jj + 1, :])
                s_ref[q, jj] = sj
                y = y + sj * r_ref[t, q, jj:jj + 1, :]
            return y

        y_ref[t] = lax.fori_loop(0, nq, update_block, jnp.zeros_like(v))
        return carry

    lax.fori_loop(0, r_ref.shape[0], step, 0)


REC_JB = 16


def _rwkv_rec(r, w, k, v, al, be, tb):
    T, N, L = r.shape
    nq = N // REC_JB
    blk = lambda a: a.reshape(T, nq, REC_JB, L)
    spec = pl.BlockSpec((tb, N, L), lambda t: (t, 0, 0))
    spec4 = pl.BlockSpec((tb, nq, REC_JB, L), lambda t: (t, 0, 0, 0))
    return pl.pallas_call(
        _rwkv_rec_kernel,
        grid=(T // tb,),
        in_specs=[spec4, spec4, spec4, spec, spec4, spec4],
        out_specs=spec,
        out_shape=jax.ShapeDtypeStruct((T, N, L), F32),
        scratch_shapes=[pltpu.VMEM((nq, REC_JB, N, L), F32)],
        compiler_params=_params(1),
        name="rwkv_rec",
    )(blk(r), blk(w), blk(k), v, blk(al), blk(be))


def _to_lanes(x):
    B, T, D = x.shape
    H = D // RWKV_HEAD
    return x.reshape(B, T, H, RWKV_HEAD).transpose(1, 3, 0, 2).reshape(T, RWKV_HEAD, B * H)


def _from_lanes(y, B):
    T, N, L = y.shape
    H = L // B
    return y.reshape(T, N, B, H).transpose(2, 0, 3, 1).reshape(B, T, H * N)


def _rwkv_post_kernel(x_ref, y_ref, r_ref, k_ref, v_ref, g_ref, lng_ref, lnb_ref, rk_ref,
                      wo_ref, e_ref, o_ref):
    e = e_ref[...]
    y = y_ref[0]
    inv_n = 1.0 / RWKV_HEAD
    d = y - _seg_sum(y, e) * inv_n
    var = _seg_sum(d * d, e) * inv_n
    yn = d * lax.rsqrt(var + GN_EPS) * lng_ref[...] + lnb_ref[...]
    bonus = _seg_sum(r_ref[0] * k_ref[0] * rk_ref[...], e) * v_ref[0]
    out = ((yn + bonus) * g_ref[0]).astype(BF16)
    o_ref[0] = x_ref[0] + _dot(out, wo_ref[...])


def _rwkv_post(x, y, r, k, v, gate, ln_g, ln_b, r_k, w_o, tm):
    B, T, D = x.shape
    row = lambda p: p.reshape(1, D)
    consts = [row(ln_g), row(ln_b), row(r_k), w_o.astype(BF16), _seg_matrix()]
    return pl.pallas_call(
        _rwkv_post_kernel,
        grid=(B, T // tm),
        in_specs=[_tile_spec(tm, D)] * 6 + [_const_spec(c.shape) for c in consts],
        out_specs=_tile_spec(tm, D),
        out_shape=jax.ShapeDtypeStruct((B, T, D), F32),
        compiler_params=_params(2),
        name="rwkv_post",
    )(x, y, r, k, v, gate, *consts)


def _conformer_kernel(kw, tm, x_ref, g_ref, win_ref, bin_ref, dw_ref, dwb_ref, lng_ref, lnb_ref,
                      wout_ref, bout_ref, o_ref, cbuf, obuf):
    D = x_ref.shape[-1]
    hdr = cbuf.shape[0] - tm
    x = x_ref[0]
    h = _rms(x, g_ref[...]).astype(BF16)
    u = _dot(h, win_ref[...]) + bin_ref[...]
    glu = u[:, :D] * _sigmoid(u[:, D:])

    @pl.when(pl.program_id(1) == 0)
    def _():
        cbuf[0:hdr, :] = jnp.zeros((hdr, D), F32)

    cbuf[hdr:hdr + tm, :] = glu
    rb = 64
    cb = 128
    base0 = hdr - (kw - 1)
    for c in range(D // cb):
        cs = slice(c * cb, (c + 1) * cb)
        for r0 in range(0, tm, rb):
            acc = None
            for s in range(SUBLANES):
                rows = rb + (SUBLANES if s else 0)
                part = None
                for k in range(kw):
                    if (base0 + k) % SUBLANES != s:
                        continue
                    a0 = r0 + base0 + k - s
                    term = dw_ref[k:k + 1, cs] * cbuf[a0:a0 + rows, cs]
                    part = term if part is None else part + term
                if part is None:
                    continue
                part = part[s:s + rb]
                acc = part if acc is None else acc + part
            obuf[r0:r0 + rb, cs] = acc
    cbuf[0:hdr, :] = cbuf[tm:tm + hdr, :]

    c = obuf[...] + dwb_ref[...]
    mu = jnp.mean(c, axis=-1, keepdims=True)
    var = jnp.mean(jnp.square(c - mu), axis=-1, keepdims=True)
    z = (c - mu) * lax.rsqrt(var + LN_EPS) * lng_ref[...] + lnb_ref[...]
    z = (z * _sigmoid(z)).astype(BF16)
    o_ref[0] = x + (_dot(z, wout_ref[...]) + bout_ref[...])


def _conformer(x, g, w_in, b_in, dw, dw_b, ln_g, ln_b, w_out, b_out, tm):
    B, T, D = x.shape
    kw = dw.shape[0]
    hdr = -(-(kw - 1) // SUBLANES) * SUBLANES
    row = lambda p: p.reshape(1, -1)
    consts = [row(g), w_in.astype(BF16), row(b_in), dw, row(dw_b), row(ln_g), row(ln_b),
              w_out.astype(BF16), row(b_out)]
    return pl.pallas_call(
        functools.partial(_conformer_kernel, kw, tm),
        grid=(B, T // tm),
        in_specs=[_tile_spec(tm, D)] + [_const_spec(c.shape) for c in consts],
        out_specs=_tile_spec(tm, D),
        out_shape=jax.ShapeDtypeStruct((B, T, D), F32),
        scratch_shapes=[pltpu.VMEM((hdr + tm, D), F32), pltpu.VMEM((tm, D), F32)],
        compiler_params=_params(2),
        name="conformer",
    )(x, *consts)


def _kv_kernel(mem_ref, g_ref, w_ref, o_ref):
    memn = _rms(mem_ref[0], g_ref[...]).astype(BF16)
    o_ref[0, 0] = _dot(memn, w_ref[0]).astype(BF16)


def _kv_proj(mem, g, w_kv):
    B, M, D = mem.shape
    L = w_kv.shape[0]
    return pl.pallas_call(
        _kv_kernel,
        grid=(L, B),
        in_specs=[pl.BlockSpec((1, M, D), lambda l, b: (b, 0, 0)),
                  pl.BlockSpec((1, D), lambda l, b: (0, 0)),
                  pl.BlockSpec((1, D, 2 * D), lambda l, b: (l, 0, 0))],
        out_specs=pl.BlockSpec((1, 1, M, 2 * D), lambda l, b: (l, b, 0, 0)),
        out_shape=jax.ShapeDtypeStruct((L, B, M, 2 * D), BF16),
        compiler_params=_params(2),
        name="kv_proj",
    )(mem, g.reshape(1, D), w_kv.astype(BF16))


def _xattn_kernel(x_ref, g_ref, wq_ref, kv_ref, wo_ref, o_ref):
    D = x_ref.shape[-1]
    hd = D // XATTN_HEADS
    x = x_ref[0]
    h = _rms(x, g_ref[...]).astype(BF16)
    q = _dot(h, wq_ref[...]).astype(BF16)
    outs = []
    for i in range(XATTN_HEADS):
        km = kv_ref[0, 0, :, i * hd:(i + 1) * hd]
        vm = kv_ref[0, 0, :, D + i * hd:D + (i + 1) * hd]
        s = lax.dot_general(q[:, i * hd:(i + 1) * hd], km, (((1,), (1,)), ((), ())),
                            preferred_element_type=F32) * (hd ** -0.5)
        s = jnp.exp(s - jnp.max(s, axis=-1, keepdims=True))
        p = (s / jnp.sum(s, axis=-1, keepdims=True)).astype(BF16)
        outs.append(_dot(p, vm).astype(BF16))
    o = jnp.concatenate(outs, axis=-1)
    o_ref[0] = x + _dot(o, wo_ref[...])


def _xattn(x, g, w_q, kv, layer, w_o, tm):
    B, T, D = x.shape
    M = kv.shape[2]
    consts = [g.reshape(1, D), w_q.astype(BF16)]
    return pl.pallas_call(
        _xattn_kernel,
        grid=(B, T // tm),
        in_specs=[_tile_spec(tm, D)] + [_const_spec(c.shape) for c in consts]
        + [pl.BlockSpec((1, 1, M, 2 * D), lambda b, t: (layer, b, 0, 0)), _const_spec((D, D))],
        out_specs=_tile_spec(tm, D),
        out_shape=jax.ShapeDtypeStruct((B, T, D), F32),
        compiler_params=_params(2),
        name="xattn",
    )(x, *consts, kv, w_o.astype(BF16))


def _ffn_kernel(kw, cw, has_final, *refs):
    if has_final:
        x_ref, g_ref, win_ref, dw_ref, wout_ref, fg_ref, o_ref, ubuf, carry = refs
    else:
        x_ref, g_ref, win_ref, dw_ref, wout_ref, o_ref, ubuf, carry = refs
    tm = x_ref.shape[1]
    F = wout_ref.shape[0]
    hdr = SUBLANES
    x = x_ref[0]
    h = _rms(x, g_ref[...]).astype(BF16)

    @pl.when(pl.program_id(1) == 0)
    def _():
        carry[...] = jnp.zeros_like(carry)

    def conv(c0):
        cs = slice(c0, c0 + cw)
        ubuf[0:hdr, :] = carry[:, cs]
        u = _dot(h, win_ref[:, cs])
        ubuf[hdr:hdr + tm, :] = u
        carry[:, cs] = u[tm - hdr:, :]
        out = dw_ref[kw - 1:kw, cs] * u
        for k in range(kw - 1):
            off = hdr - (kw - 1) + k
            out = out + dw_ref[k:k + 1, cs] * ubuf[off:off + tm, :]
        return out

    acc = None
    for c0 in range(0, F, cw):
        gate = conv(c0)
        val = conv(F + c0)
        act = (gate * _sigmoid(gate) * val).astype(BF16)
        part = _dot(act, wout_ref[c0:c0 + cw, :])
        acc = part if acc is None else acc + part
    y = x + acc
    if has_final:
        y = _rms(y, fg_ref[...])
    o_ref[0] = y


def _ffn(x, g, w_in, dw, w_out, final_g, tm, cw):
    B, T, D = x.shape
    kw = dw.shape[0]
    F = w_out.shape[0]
    has_final = final_g is not None
    consts = [g.reshape(1, D), w_in.astype(BF16), dw, w_out.astype(BF16)]
    if has_final:
        consts.append(final_g.reshape(1, D))
    return pl.pallas_call(
        functools.partial(_ffn_kernel, kw, cw, has_final),
        grid=(B, T // tm),
        in_specs=[_tile_spec(tm, D)] + [_const_spec(c.shape) for c in consts],
        out_specs=_tile_spec(tm, D),
        out_shape=jax.ShapeDtypeStruct((B, T, D), F32),
        scratch_shapes=[pltpu.VMEM((SUBLANES + tm, cw), F32), pltpu.VMEM((SUBLANES, 2 * F), F32)],
        compiler_params=_params(2),
        name="ffn",
    )(x, *consts)


TM = 256
TM_WIDE = 512
REC_TB = 64
FFN_CW = 256


def rwkv_layer(x, g, mu, w_r, w_k, w_v, w_o, w0, w1, w2, a0, a1, a2, g1, g2, k_k, k_a, r_k,
               ln_g, ln_b, v_first, v_res, tm, tb):
    r, w, k, v, al, be, gate = _rwkv_pre(x, g, mu, w_r, w_k, w_v, w0, w1, w2, a0, a1, a2, g1, g2,
                                         k_k, k_a, v_first, v_res, tm)
    y = _rwkv_rec(*(_to_lanes(a) for a in (r, w, k, v, al, be)), tb)
    x = _rwkv_post(x, _from_lanes(y, x.shape[0]), r, k, v, gate, ln_g, ln_b, r_k.reshape(-1), w_o, tm)
    return x, v


def kernel(x, mem, mem_norm_g, norm_mix_g, norm_xattn_g, norm_ffn_g, final_norm_g, rwkv_mu, rwkv_w_r, rwkv_w_k, rwkv_w_v, rwkv_w_o, rwkv_w0, rwkv_w1, rwkv_w2, rwkv_a0, rwkv_a1, rwkv_a2, rwkv_g1, rwkv_g2, rwkv_k_k, rwkv_k_a, rwkv_r_k, rwkv_ln_g, rwkv_ln_b, rwkv_v0, rwkv_v1, rwkv_v2, conv_w_in, conv_b_in, conv_dw, conv_dw_b, conv_ln_g, conv_ln_b, conv_w_out, conv_b_out, xattn_w_q, xattn_w_kv, xattn_w_o, ffn_w_in, ffn_dw, ffn_w_out):
    B, T, D = x.shape
    depth = norm_mix_g.shape[0]
    tm = min(TM, T)
    tmw = min(TM_WIDE, T)
    kv = _kv_proj(mem, mem_norm_g, xattn_w_kv)
    v_first = None
    ia = 0
    ib = 0
    for layer in range(depth):
        if layer % 2 == 0:
            v_res = None if ia == 0 else (rwkv_v0[ia - 1], rwkv_v1[ia - 1], rwkv_v2[ia - 1])
            x, v = rwkv_layer(
                x, norm_mix_g[layer], rwkv_mu[ia], rwkv_w_r[ia], rwkv_w_k[ia], rwkv_w_v[ia],
                rwkv_w_o[ia], rwkv_w0[ia], rwkv_w1[ia], rwkv_w2[ia], rwkv_a0[ia], rwkv_a1[ia],
                rwkv_a2[ia], rwkv_g1[ia], rwkv_g2[ia], rwkv_k_k[ia], rwkv_k_a[ia], rwkv_r_k[ia],
                rwkv_ln_g[ia], rwkv_ln_b[ia], v_first, v_res, tm, min(REC_TB, T))
            if ia == 0:
                v_first = v
            ia += 1
        else:
            x = _conformer(x, norm_mix_g[layer], conv_w_in[ib], conv_b_in[ib], conv_dw[ib],
                           conv_dw_b[ib], conv_ln_g[ib], conv_ln_b[ib], conv_w_out[ib],
                           conv_b_out[ib], tm)
            ib += 1
        x = _xattn(x, norm_xattn_g[layer], xattn_w_q[layer], kv, layer, xattn_w_o[layer], tmw)
        x = _ffn(x, norm_ffn_g[layer], ffn_w_in[layer], ffn_dw[layer], ffn_w_out[layer],
                 final_norm_g if layer == depth - 1 else None, tmw, FFN_CW)
    return x
```

```python
import functools

import jax
import jax.numpy as jnp
from jax import lax
from jax.experimental import pallas as pl
from jax.experimental.pallas import tpu as pltpu

F32 = jnp.float32
BF16 = jnp.bfloat16

RWKV_HEAD = 64
GN_EPS = 64e-5
NORM_EPS = 1e-6
LN_EPS = 1e-5
XATTN_HEADS = 4
SUBLANES = 8
SEG_BLOCK = 256
GROUP_LANES = SEG_BLOCK
VMEM_LIMIT = 56 * 1024 * 1024


def _params(n_grid):
    return pltpu.CompilerParams(
        dimension_semantics=("arbitrary",) * n_grid, vmem_limit_bytes=VMEM_LIMIT)


def _dot(a, b):
    return jnp.dot(a, b, preferred_element_type=F32)


def _rms(x, g):
    return x * lax.rsqrt(jnp.mean(x * x, axis=-1, keepdims=True) + NORM_EPS) * g


def _sigmoid(x):
    return 1.0 / (1.0 + jnp.exp(-x))


def _softplus(x):
    return jnp.maximum(x, 0.0) + jnp.log(1.0 + jnp.exp(-jnp.abs(x)))


def _seg_sum(y, e):
    outs = []
    for c in range(y.shape[-1] // SEG_BLOCK):
        yb = y[:, c * SEG_BLOCK:(c + 1) * SEG_BLOCK]
        hi = yb.astype(BF16)
        lo = (yb - hi.astype(F32)).astype(BF16)
        outs.append(_dot(hi, e) + _dot(lo, e))
    return jnp.concatenate(outs, axis=-1)


def _seg_matrix():
    i = lax.broadcasted_iota(jnp.int32, (SEG_BLOCK, SEG_BLOCK), 0) // RWKV_HEAD
    j = lax.broadcasted_iota(jnp.int32, (SEG_BLOCK, SEG_BLOCK), 1) // RWKV_HEAD
    return (i == j).astype(BF16)


def _const_spec(shape):
    zeros = (0,) * len(shape)
    return pl.BlockSpec(shape, lambda *_: zeros, pipeline_mode=pl.Buffered(1))


def _tile_spec(tm, d):
    return pl.BlockSpec((1, tm, d), lambda b, t: (b, t, 0))


def _prev_rows_spec(tm, d):
    per = tm // SUBLANES
    return pl.BlockSpec((1, SUBLANES, d), lambda b, t: (b, jnp.maximum(t * per - 1, 0), 0))


def _rwkv_pre_kernel(has_vres, *refs):
    if has_vres:
        (x_ref, xp_ref, g_ref, mu_ref, wr_ref, wk_ref, wv_ref, w0_ref, w1_ref, w2_ref,
         a0_ref, a1_ref, a2_ref, g1_ref, g2_ref, kk_ref, ka_ref, e_ref,
         vf_ref, v0_ref, v1_ref, v2_ref,
         r_out, w_out, k_out, v_out, al_out, be_out, g_out) = refs
    else:
        (x_ref, xp_ref, g_ref, mu_ref, wr_ref, wk_ref, wv_ref, w0_ref, w1_ref, w2_ref,
         a0_ref, a1_ref, a2_ref, g1_ref, g2_ref, kk_ref, ka_ref, e_ref,
         r_out, w_out, k_out, v_out, al_out, be_out, g_out) = refs
    t = pl.program_id(1)
    g = g_ref[...]
    h = _rms(x_ref[0], g)
    hp = _rms(xp_ref[0], g)
    prev = jnp.where(t > 0, hp[SUBLANES - 1:SUBLANES, :], 0.0)
    row = lax.broadcasted_iota(jnp.int32, h.shape, 0)
    hs = jnp.where(row == 0, prev, pltpu.roll(h, 1, axis=0))
    xx = hs - h

    def mix(i):
        return (h + xx * mu_ref[i:i + 1, :]).astype(BF16)

    e = e_ref[...]
    r = _dot(mix(0), wr_ref[...])
    wl = _dot(jnp.tanh(_dot(mix(1), w1_ref[...])).astype(BF16), w2_ref[...])
    k = _dot(mix(2), wk_ref[...])
    xv = mix(3)
    v = _dot(xv, wv_ref[...])
    al = _dot(_dot(mix(4), a1_ref[...]).astype(BF16), a2_ref[...])
    gate = _dot(_sigmoid(_dot(mix(5), g1_ref[...])).astype(BF16), g2_ref[...])

    logw = -_softplus(-(w0_ref[...] + wl)) - 0.5
    log_decay = -jnp.exp(logw)
    a = _sigmoid(a0_ref[...] + al)
    kk = k * kk_ref[...]
    kk = kk / jnp.maximum(jnp.sqrt(_seg_sum(kk * kk, e)), 1e-12)
    k = k * (1.0 + (a - 1.0) * ka_ref[...])
    if has_vres:
        vl = _dot(_dot(xv, v1_ref[...]).astype(BF16), v2_ref[...])
        v = v + (vf_ref[0] - v) * _sigmoid(v0_ref[...] + vl)
    r_out[0] = r
    w_out[0] = log_decay
    k_out[0] = k
    v_out[0] = v
    al_out[0] = -kk
    be_out[0] = kk * a
    g_out[0] = gate


def _rwkv_pre(x, g, mu, wr, wk, wv, w0, w1, w2, a0, a1, a2, g1, g2, k_k, k_a, v_first, v_res, tm):
    B, T, D = x.shape
    has_vres = v_res is not None
    row = lambda p: p.reshape(1, D)
    bf = lambda p: p.astype(BF16)
    args = [x, x, row(g), mu, bf(wr), bf(wk), bf(wv), row(w0), bf(w1), bf(w2),
            row(a0), bf(a1), bf(a2), bf(g1), bf(g2), row(k_k), row(k_a), _seg_matrix()]
    specs = [_tile_spec(tm, D), _prev_rows_spec(tm, D)] + [_const_spec(a.shape) for a in args[2:]]
    if has_vres:
        v0, v1, v2 = v_res
        extra = [v_first, row(v0), bf(v1), bf(v2)]
        args += extra
        specs += [_tile_spec(tm, D)] + [_const_spec(a.shape) for a in extra[1:]]
    out = jax.ShapeDtypeStruct((B, T, D), F32)
    return pl.pallas_call(
        functools.partial(_rwkv_pre_kernel, has_vres),
        grid=(B, T // tm),
        in_specs=specs,
        out_specs=[_tile_spec(tm, D)] * 7,
        out_shape=[out] * 7,
        compiler_params=_params(2),
        name="rwkv_pre",
    )(*args)


def _dot_nt(a, b):
    return lax.dot_general(a, b, (((1,), (1,)), ((), ())), preferred_element_type=F32)


def _dot_tn(a, b):
    return lax.dot_general(a, b, (((0,), (0,)), ((), ())), preferred_element_type=F32)


def _rwkv_chunk_kernel(r_ref, lw_ref, k_ref, v_ref, al_ref, be_ref, tril_ref, bdm_ref, y_ref, z_ref):
    L = r_ref.shape[1]
    D = r_ref.shape[2]
    reps = GROUP_LANES // L

    @pl.when(pl.program_id(1) == 0)
    def _():
        z_ref[...] = jnp.zeros_like(z_ref)

    tril = tril_ref[...]
    bdm = bdm_ref[...]

    def bd(side):
        return jnp.concatenate([side.astype(BF16)] * reps, axis=0) * bdm

    lw = lw_ref[0]
    hi = lw.astype(BF16)
    rest = lw - hi.astype(F32)
    mid = rest.astype(BF16)
    lo = (rest - mid.astype(F32)).astype(BF16)
    cum = _dot(tril, hi) + _dot(tril, mid) + _dot(tril, lo)
    cum_last = cum[L - 1:L, :]
    e_neg = jnp.exp(-cum)
    e_end = jnp.exp(cum_last - cum)
    at_all = al_ref[0] * jnp.exp(cum - lw)
    rt_all = r_ref[0] * jnp.exp(cum)
    bt_all = be_ref[0] * e_neg
    kt_all = k_ref[0] * e_neg
    bh_all = be_ref[0] * e_end
    kh_all = k_ref[0] * e_end
    p_last = jnp.exp(cum_last)

    trow = lax.broadcasted_iota(jnp.int32, (L, GROUP_LANES), 0)
    tcol = lax.broadcasted_iota(jnp.int32, (L, GROUP_LANES), 1) % L
    strict = tcol < trow
    incl = tcol <= trow
    eye = jnp.where(tcol == trow, 1.0, 0.0)

    groups = range(D // GROUP_LANES)
    gsl = [slice(g * GROUP_LANES, (g + 1) * GROUP_LANES) for g in groups]
    bdm32 = bdm.astype(F32)
    at = [at_all[:, s] for s in gsl]
    rt = [rt_all[:, s].astype(BF16) for s in gsl]
    v = [v_ref[0, :, s] for s in gsl]
    gm = [_dot_nt(jnp.concatenate([at[g].astype(BF16), rt[g]], axis=0),
                  jnp.concatenate([bd(bt_all[:, gsl[g]]), bd(kt_all[:, gsl[g]])], axis=0)) for g in groups]
    m = [jnp.where(strict, gm[g][:L, :GROUP_LANES], 0.0) for g in groups]
    aak = [jnp.where(strict, gm[g][:L, GROUP_LANES:], 0.0).astype(BF16) for g in groups]
    arb = [jnp.where(incl, gm[g][L:, :GROUP_LANES], 0.0).astype(BF16) for g in groups]
    ark = [jnp.where(incl, gm[g][L:, GROUP_LANES:], 0.0).astype(BF16) for g in groups]
    x = [eye + m[g] for g in groups]
    for _ in range(L.bit_length() - 2):
        m = [_dot(m[g].astype(BF16), bd(m[g])) for g in groups]
        x = [x[g] + _dot(m[g].astype(BF16), bd(x[g])) for g in groups]
    xb = [x[g].astype(BF16) for g in groups]
    bdv = [bd(v[g]) for g in groups]
    ap = [_dot(xb[g], bd(at[g])).astype(BF16) for g in groups]
    w1 = [_dot(aak[g], bdv[g]) for g in groups]
    u0 = [_dot(xb[g], bd(w1[g])) for g in groups]
    zt = [z_ref[g] for g in groups]
    ztb = [zt[g].astype(BF16) for g in groups]
    u = [_dot_nt(ap[g], ztb[g]) + u0[g] for g in groups]
    y = [_dot_nt(rt[g], ztb[g]) + _dot(arb[g], bd(u[g])) + _dot(ark[g], bdv[g]) for g in groups]
    upd = [_dot_tn(jnp.concatenate([u[g], v[g]], axis=0).astype(BF16),
                   jnp.concatenate([bh_all[:, gsl[g]], kh_all[:, gsl[g]]], axis=0).astype(BF16))
           for g in groups]
    for g in groups:
        y_ref[0, :, gsl[g]] = y[g]
        z_ref[g] = p_last[:, gsl[g]] * zt[g] + upd[g] * bdm32


def _rwkv_chunk(r, lw, k, v, al, be, chunk):
    B, T, D = r.shape
    i = lax.broadcasted_iota(jnp.int32, (chunk, chunk), 0)
    j = lax.broadcasted_iota(jnp.int32, (chunk, chunk), 1)
    consts = [(j <= i).astype(BF16), _seg_matrix()]
    return pl.pallas_call(
        _rwkv_chunk_kernel,
        grid=(B, T // chunk),
        in_specs=[_tile_spec(chunk, D)] * 6 + [_const_spec(c.shape) for c in consts],
        out_specs=_tile_spec(chunk, D),
        out_shape=jax.ShapeDtypeStruct((B, T, D), F32),
        scratch_shapes=[pltpu.VMEM((D // GROUP_LANES, GROUP_LANES, GROUP_LANES), F32)],
        compiler_params=_params(2),
        name="rwkv_chunk",
    )(r, lw, k, v, al, be, *consts)


def _rwkv_post_kernel(x_ref, y_ref, r_ref, k_ref, v_ref, g_ref, lng_ref, lnb_ref, rk_ref,
                      wo_ref, e_ref, o_ref):
    e = e_ref[...]
    y = y_ref[0]
    inv_n = 1.0 / RWKV_HEAD
    d = y - _seg_sum(y, e) * inv_n
    var = _seg_sum(d * d, e) * inv_n
    yn = d * lax.rsqrt(var + GN_EPS) * lng_ref[...] + lnb_ref[...]
    bonus = _seg_sum(r_ref[0] * k_ref[0] * rk_ref[...], e) * v_ref[0]
    out = ((yn + bonus) * g_ref[0]).astype(BF16)
    o_ref[0] = x_ref[0] + _dot(out, wo_ref[...])


def _rwkv_post(x, y, r, k, v, gate, ln_g, ln_b, r_k, w_o, tm):
    B, T, D = x.shape
    row = lambda p: p.reshape(1, D)
    consts = [row(ln_g), row(ln_b), row(r_k), w_o.astype(BF16), _seg_matrix()]
    return pl.pallas_call(
        _rwkv_post_kernel,
        grid=(B, T // tm),
        in_specs=[_tile_spec(tm, D)] * 6 + [_const_spec(c.shape) for c in consts],
        out_specs=_tile_spec(tm, D),
        out_shape=jax.ShapeDtypeStruct((B, T, D), F32),
        compiler_params=_params(2),
        name="rwkv_post",
    )(x, y, r, k, v, gate, *consts)


def _conformer_kernel(kw, tm, x_ref, g_ref, win_ref, bin_ref, dw_ref, dwb_ref, lng_ref, lnb_ref,
                      wout_ref, bout_ref, o_ref, cbuf, obuf):
    D = x_ref.shape[-1]
    hdr = cbuf.shape[0] - tm
    x = x_ref[0]
    h = _rms(x, g_ref[...]).astype(BF16)
    u = _dot(h, win_ref[...]) + bin_ref[...]
    glu = u[:, :D] * _sigmoid(u[:, D:])

    @pl.when(pl.program_id(1) == 0)
    def _():
        cbuf[0:hdr, :] = jnp.zeros((hdr, D), F32)

    cbuf[hdr:hdr + tm, :] = glu
    rb = 64
    cb = 128
    base0 = hdr - (kw - 1)
    for c in range(D // cb):
        cs = slice(c * cb, (c + 1) * cb)
        for r0 in range(0, tm, rb):
            acc = None
            for s in range(SUBLANES):
                rows = rb + (SUBLANES if s else 0)
                part = None
                for k in range(kw):
                    if (base0 + k) % SUBLANES != s:
                        continue
                    a0 = r0 + base0 + k - s
                    term = dw_ref[k:k + 1, cs] * cbuf[a0:a0 + rows, cs]
                    part = term if part is None else part + term
                if part is None:
                    continue
                part = part[s:s + rb]
                acc = part if acc is None else acc + part
            obuf[r0:r0 + rb, cs] = acc
    cbuf[0:hdr, :] = cbuf[tm:tm + hdr, :]

    c = obuf[...] + dwb_ref[...]
    mu = jnp.mean(c, axis=-1, keepdims=True)
    var = jnp.mean(jnp.square(c - mu), axis=-1, keepdims=True)
    z = (c - mu) * lax.rsqrt(var + LN_EPS) * lng_ref[...] + lnb_ref[...]
    z = (z * _sigmoid(z)).astype(BF16)
    o_ref[0] = x + (_dot(z, wout_ref[...]) + bout_ref[...])


def _conformer(x, g, w_in, b_in, dw, dw_b, ln_g, ln_b, w_out, b_out, tm):
    B, T, D = x.shape
    kw = dw.shape[0]
    hdr = -(-(kw - 1) // SUBLANES) * SUBLANES
    row = lambda p: p.reshape(1, -1)
    consts = [row(g), w_in.astype(BF16), row(b_in), dw, row(dw_b), row(ln_g), row(ln_b),
              w_out.astype(BF16), row(b_out)]
    return pl.pallas_call(
        functools.partial(_conformer_kernel, kw, tm),
        grid=(B, T // tm),
        in_specs=[_tile_spec(tm, D)] + [_const_spec(c.shape) for c in consts],
        out_specs=_tile_spec(tm, D),
        out_shape=jax.ShapeDtypeStruct((B, T, D), F32),
        scratch_shapes=[pltpu.VMEM((hdr + tm, D), F32), pltpu.VMEM((tm, D), F32)],
        compiler_params=_params(2),
        name="conformer",
    )(x, *consts)


def _kv_kernel(mem_ref, g_ref, w_ref, o_ref):
    memn = _rms(mem_ref[0], g_ref[...]).astype(BF16)
    o_ref[0, 0] = _dot(memn, w_ref[0]).astype(BF16)


def _kv_proj(mem, g, w_kv):
    B, M, D = mem.shape
    L = w_kv.shape[0]
    return pl.pallas_call(
        _kv_kernel,
        grid=(L, B),
        in_specs=[pl.BlockSpec((1, M, D), lambda l, b: (b, 0, 0)),
                  pl.BlockSpec((1, D), lambda l, b: (0, 0)),
                  pl.BlockSpec((1, D, 2 * D), lambda l, b: (l, 0, 0))],
        out_specs=pl.BlockSpec((1, 1, M, 2 * D), lambda l, b: (l, b, 0, 0)),
        out_shape=jax.ShapeDtypeStruct((L, B, M, 2 * D), BF16),
        compiler_params=_params(2),
        name="kv_proj",
    )(mem, g.reshape(1, D), w_kv.astype(BF16))


def _xattn_kernel(x_ref, g_ref, wq_ref, kv_ref, wo_ref, o_ref):
    D = x_ref.shape[-1]
    hd = D // XATTN_HEADS
    x = x_ref[0]
    h = _rms(x, g_ref[...]).astype(BF16)
    q = _dot(h, wq_ref[...]).astype(BF16)
    outs = []
    for i in range(XATTN_HEADS):
        km = kv_ref[0, 0, :, i * hd:(i + 1) * hd]
        vm = kv_ref[0, 0, :, D + i * hd:D + (i + 1) * hd]
        s = lax.dot_general(q[:, i * hd:(i + 1) * hd], km, (((1,), (1,)), ((), ())),
                            preferred_element_type=F32) * (hd ** -0.5)
        s = jnp.exp(s - jnp.max(s, axis=-1, keepdims=True))
        p = (s / jnp.sum(s, axis=-1, keepdims=True)).astype(BF16)
        outs.append(_dot(p, vm).astype(BF16))
    o = jnp.concatenate(outs, axis=-1)
    o_ref[0] = x + _dot(o, wo_ref[...])


def _xattn(x, g, w_q, kv, layer, w_o, tm):
    B, T, D = x.shape
    M = kv.shape[2]
    consts = [g.reshape(1, D), w_q.astype(BF16)]
    return pl.pallas_call(
        _xattn_kernel,
        grid=(B, T // tm),
        in_specs=[_tile_spec(tm, D)] + [_const_spec(c.shape) for c in consts]
        + [pl.BlockSpec((1, 1, M, 2 * D), lambda b, t: (layer, b, 0, 0)), _const_spec((D, D))],
        out_specs=_tile_spec(tm, D),
        out_shape=jax.ShapeDtypeStruct((B, T, D), F32),
        compiler_params=_params(2),
        name="xattn",
    )(x, *consts, kv, w_o.astype(BF16))


def _ffn_kernel(kw, cw, has_final, *refs):
    if has_final:
        x_ref, g_ref, win_ref, dw_ref, wout_ref, fg_ref, o_ref, ubuf, carry = refs
    else:
        x_ref, g_ref, win_ref, dw_ref, wout_ref, o_ref, ubuf, carry = refs
    tm = x_ref.shape[1]
    F = wout_ref.shape[0]
    hdr = SUBLANES
    x = x_ref[0]
    h = _rms(x, g_ref[...]).astype(BF16)

    @pl.when(pl.program_id(1) == 0)
    def _():
        carry[...] = jnp.zeros_like(carry)

    def conv(c0):
        cs = slice(c0, c0 + cw)
        ubuf[0:hdr, :] = carry[:, cs]
        u = _dot(h, win_ref[:, cs])
        ubuf[hdr:hdr + tm, :] = u
        carry[:, cs] = u[tm - hdr:, :]
        out = dw_ref[kw - 1:kw, cs] * u
        for k in range(kw - 1):
            off = hdr - (kw - 1) + k
            out = out + dw_ref[k:k + 1, cs] * ubuf[off:off + tm, :]
        return out

    acc = None
    for c0 in range(0, F, cw):
        gate = conv(c0)
        val = conv(F + c0)
        act = (gate * _sigmoid(gate) * val).astype(BF16)
        part = _dot(act, wout_ref[c0:c0 + cw, :])
        acc = part if acc is None else acc + part
    y = x + acc
    if has_final:
        y = _rms(y, fg_ref[...])
    o_ref[0] = y


def _ffn(x, g, w_in, dw, w_out, final_g, tm, cw):
    B, T, D = x.shape
    kw = dw.shape[0]
    F = w_out.shape[0]
    has_final = final_g is not None
    consts = [g.reshape(1, D), w_in.astype(BF16), dw, w_out.astype(BF16)]
    if has_final:
        consts.append(final_g.reshape(1, D))
    return pl.pallas_call(
        functools.partial(_ffn_kernel, kw, cw, has_final),
        grid=(B, T // tm),
        in_specs=[_tile_spec(tm, D)] + [_const_spec(c.shape) for c in consts],
        out_specs=_tile_spec(tm, D),
        out_shape=jax.ShapeDtypeStruct((B, T, D), F32),
        scratch_shapes=[pltpu.VMEM((SUBLANES + tm, cw), F32), pltpu.VMEM((SUBLANES, 2 * F), F32)],
        compiler_params=_params(2),
        name="ffn",
    )(x, *consts)


TM = 256
TM_WIDE = 512
REC_CHUNK = 64
FFN_CW = 256


def rwkv_layer(x, g, mu, w_r, w_k, w_v, w_o, w0, w1, w2, a0, a1, a2, g1, g2, k_k, k_a, r_k,
               ln_g, ln_b, v_first, v_res, tm, tb):
    r, w, k, v, al, be, gate = _rwkv_pre(x, g, mu, w_r, w_k, w_v, w0, w1, w2, a0, a1, a2, g1, g2,
                                         k_k, k_a, v_first, v_res, tm)
    y = _rwkv_chunk(r, w, k, v, al, be, tb)
    x = _rwkv_post(x, y, r, k, v, gate, ln_g, ln_b, r_k.reshape(-1), w_o, tm)
    return x, v


def kernel(x, mem, mem_norm_g, norm_mix_g, norm_xattn_g, norm_ffn_g, final_norm_g, rwkv_mu, rwkv_w_r, rwkv_w_k, rwkv_w_v, rwkv_w_o, rwkv_w0, rwkv_w1, rwkv_w2, rwkv_a0, rwkv_a1, rwkv_a2, rwkv_g1, rwkv_g2, rwkv_k_k, rwkv_k_a, rwkv_r_k, rwkv_ln_g, rwkv_ln_b, rwkv_v0, rwkv_v1, rwkv_v2, conv_w_in, conv_b_in, conv_dw, conv_dw_b, conv_ln_g, conv_ln_b, conv_w_out, conv_b_out, xattn_w_q, xattn_w_kv, xattn_w_o, ffn_w_in, ffn_dw, ffn_w_out):
    B, T, D = x.shape
    depth = norm_mix_g.shape[0]
    tm = min(TM, T)
    tmw = min(TM_WIDE, T)
    kv = _kv_proj(mem, mem_norm_g, xattn_w_kv)
    v_first = None
    ia = 0
    ib = 0
    for layer in range(depth):
        if layer % 2 == 0:
            v_res = None if ia == 0 else (rwkv_v0[ia - 1], rwkv_v1[ia - 1], rwkv_v2[ia - 1])
            x, v = rwkv_layer(
                x, norm_mix_g[layer], rwkv_mu[ia], rwkv_w_r[ia], rwkv_w_k[ia], rwkv_w_v[ia],
                rwkv_w_o[ia], rwkv_w0[ia], rwkv_w1[ia], rwkv_w2[ia], rwkv_a0[ia], rwkv_a1[ia],
                rwkv_a2[ia], rwkv_g1[ia], rwkv_g2[ia], rwkv_k_k[ia], rwkv_k_a[ia], rwkv_r_k[ia],
                rwkv_ln_g[ia], rwkv_ln_b[ia], v_first, v_res, tm, min(REC_CHUNK, T))
            if ia == 0:
                v_first = v
            ia += 1
        else:
            x = _conformer(x, norm_mix_g[layer], conv_w_in[ib], conv_b_in[ib], conv_dw[ib],
                           conv_dw_b[ib], conv_ln_g[ib], conv_ln_b[ib], conv_w_out[ib],
                           conv_b_out[ib], tm)
            ib += 1
        x = _xattn(x, norm_xattn_g[layer], xattn_w_q[layer], kv, layer, xattn_w_o[layer], tmw)
        x = _ffn(x, norm_ffn_g[layer], ffn_w_in[layer], ffn_dw[layer], ffn_w_out[layer],
                 final_norm_g if layer == depth - 1 else None, tmw, FFN_CW)
    return x
```

```python
import functools

import jax
import jax.numpy as jnp
from jax import lax
from jax.experimental import pallas as pl
from jax.experimental.pallas import tpu as pltpu

F32 = jnp.float32
BF16 = jnp.bfloat16

RWKV_HEAD = 64
GN_EPS = 64e-5
NORM_EPS = 1e-6
LN_EPS = 1e-5
XATTN_HEADS = 4
SUBLANES = 8
SEG_BLOCK = 256
GROUP_LANES = SEG_BLOCK
VMEM_LIMIT = 56 * 1024 * 1024


def _params(n_grid):
    return pltpu.CompilerParams(
        dimension_semantics=("arbitrary",) * n_grid, vmem_limit_bytes=VMEM_LIMIT)


def _dot(a, b):
    return jnp.dot(a, b, preferred_element_type=F32)


def _rms(x, g):
    return x * lax.rsqrt(jnp.mean(x * x, axis=-1, keepdims=True) + NORM_EPS) * g


def _sigmoid(x):
    return 1.0 / (1.0 + jnp.exp(-x))


def _softplus(x):
    return jnp.maximum(x, 0.0) + jnp.log(1.0 + jnp.exp(-jnp.abs(x)))


def _seg_sum(y, e):
    outs = []
    for c in range(y.shape[-1] // SEG_BLOCK):
        yb = y[:, c * SEG_BLOCK:(c + 1) * SEG_BLOCK]
        hi = yb.astype(BF16)
        lo = (yb - hi.astype(F32)).astype(BF16)
        outs.append(_dot(hi, e) + _dot(lo, e))
    return jnp.concatenate(outs, axis=-1)


def _seg_matrix():
    i = lax.broadcasted_iota(jnp.int32, (SEG_BLOCK, SEG_BLOCK), 0) // RWKV_HEAD
    j = lax.broadcasted_iota(jnp.int32, (SEG_BLOCK, SEG_BLOCK), 1) // RWKV_HEAD
    return (i == j).astype(BF16)


def _const_spec(shape):
    zeros = (0,) * len(shape)
    return pl.BlockSpec(shape, lambda *_: zeros, pipeline_mode=pl.Buffered(1))


def _tile_spec(tm, d):
    return pl.BlockSpec((1, tm, d), lambda b, t: (b, t, 0))


def _prev_rows_spec(tm, d):
    per = tm // SUBLANES
    return pl.BlockSpec((1, SUBLANES, d), lambda b, t: (b, jnp.maximum(t * per - 1, 0), 0))


def _rwkv_pre_kernel(has_vres, *refs):
    if has_vres:
        (x_ref, xp_ref, g_ref, mu_ref, wr_ref, wk_ref, wv_ref, w0_ref, w1_ref, w2_ref,
         a0_ref, a1_ref, a2_ref, g1_ref, g2_ref, kk_ref, ka_ref, e_ref,
         vf_ref, v0_ref, v1_ref, v2_ref,
         r_out, w_out, k_out, v_out, al_out, be_out, g_out) = refs
    else:
        (x_ref, xp_ref, g_ref, mu_ref, wr_ref, wk_ref, wv_ref, w0_ref, w1_ref, w2_ref,
         a0_ref, a1_ref, a2_ref, g1_ref, g2_ref, kk_ref, ka_ref, e_ref,
         r_out, w_out, k_out, v_out, al_out, be_out, g_out) = refs
    t = pl.program_id(1)
    g = g_ref[...]
    h = _rms(x_ref[0], g)
    hp = _rms(xp_ref[0], g)
    prev = jnp.where(t > 0, hp[SUBLANES - 1:SUBLANES, :], 0.0)
    row = lax.broadcasted_iota(jnp.int32, h.shape, 0)
    hs = jnp.where(row == 0, prev, pltpu.roll(h, 1, axis=0))
    xx = hs - h

    def mix(i):
        return (h + xx * mu_ref[i:i + 1, :]).astype(BF16)

    e = e_ref[...]
    r = _dot(mix(0), wr_ref[...])
    wl = _dot(jnp.tanh(_dot(mix(1), w1_ref[...])).astype(BF16), w2_ref[...])
    k = _dot(mix(2), wk_ref[...])
    xv = mix(3)
    v = _dot(xv, wv_ref[...])
    al = _dot(_dot(mix(4), a1_ref[...]).astype(BF16), a2_ref[...])
    gate = _dot(_sigmoid(_dot(mix(5), g1_ref[...])).astype(BF16), g2_ref[...])

    logw = -_softplus(-(w0_ref[...] + wl)) - 0.5
    log_decay = -jnp.exp(logw)
    a = _sigmoid(a0_ref[...] + al)
    kk = k * kk_ref[...]
    kk = kk / jnp.maximum(jnp.sqrt(_seg_sum(kk * kk, e)), 1e-12)
    k = k * (1.0 + (a - 1.0) * ka_ref[...])
    if has_vres:
        vl = _dot(_dot(xv, v1_ref[...]).astype(BF16), v2_ref[...])
        v = v + (vf_ref[0] - v) * _sigmoid(v0_ref[...] + vl)
    r_out[0] = r
    w_out[0] = log_decay
    k_out[0] = k
    v_out[0] = v
    al_out[0] = -kk
    be_out[0] = kk * a
    g_out[0] = gate


def _rwkv_pre(x, g, mu, wr, wk, wv, w0, w1, w2, a0, a1, a2, g1, g2, k_k, k_a, v_first, v_res, tm):
    B, T, D = x.shape
    has_vres = v_res is not None
    row = lambda p: p.reshape(1, D)
    bf = lambda p: p.astype(BF16)
    args = [x, x, row(g), mu, bf(wr), bf(wk), bf(wv), row(w0), bf(w1), bf(w2),
            row(a0), bf(a1), bf(a2), bf(g1), bf(g2), row(k_k), row(k_a), _seg_matrix()]
    specs = [_tile_spec(tm, D), _prev_rows_spec(tm, D)] + [_const_spec(a.shape) for a in args[2:]]
    if has_vres:
        v0, v1, v2 = v_res
        extra = [v_first, row(v0), bf(v1), bf(v2)]
        args += extra
        specs += [_tile_spec(tm, D)] + [_const_spec(a.shape) for a in extra[1:]]
    out = jax.ShapeDtypeStruct((B, T, D), F32)
    return pl.pallas_call(
        functools.partial(_rwkv_pre_kernel, has_vres),
        grid=(B, T // tm),
        in_specs=specs,
        out_specs=[_tile_spec(tm, D)] * 7,
        out_shape=[out] * 7,
        compiler_params=_params(2),
        name="rwkv_pre",
    )(*args)


def _dot_nt(a, b):
    return lax.dot_general(a, b, (((1,), (1,)), ((), ())), preferred_element_type=F32)


def _dot_tn(a, b):
    return lax.dot_general(a, b, (((0,), (0,)), ((), ())), preferred_element_type=F32)


def _rwkv_chunk_kernel(r_ref, lw_ref, k_ref, v_ref, al_ref, be_ref, tril_ref, bdm_ref, y_ref, z_ref):
    nb, L, D = r_ref.shape
    reps = GROUP_LANES // L

    @pl.when(pl.program_id(1) == 0)
    def _():
        z_ref[...] = jnp.zeros_like(z_ref)

    tril = tril_ref[...]
    bdm = bdm_ref[...]

    def bd(side):
        return jnp.concatenate([side.astype(BF16)] * reps, axis=0) * bdm

    at_all, rt_all, bt_all, kt_all, bh_all, kh_all, p_last = [], [], [], [], [], [], []
    for bb in range(nb):
        lw = lw_ref[bb]
        hi = lw.astype(BF16)
        rest = lw - hi.astype(F32)
        mid = rest.astype(BF16)
        lo = (rest - mid.astype(F32)).astype(BF16)
        cum = _dot(tril, hi) + _dot(tril, mid) + _dot(tril, lo)
        cum_last = cum[L - 1:L, :]
        e_neg = jnp.exp(-cum)
        e_end = jnp.exp(cum_last - cum)
        at_all.append(al_ref[bb] * jnp.exp(cum - lw))
        rt_all.append(r_ref[bb] * jnp.exp(cum))
        bt_all.append(be_ref[bb] * e_neg)
        kt_all.append(k_ref[bb] * e_neg)
        bh_all.append(be_ref[bb] * e_end)
        kh_all.append(k_ref[bb] * e_end)
        p_last.append(jnp.exp(cum_last))

    trow = lax.broadcasted_iota(jnp.int32, (L, GROUP_LANES), 0)
    tcol = lax.broadcasted_iota(jnp.int32, (L, GROUP_LANES), 1) % L
    strict = tcol < trow
    incl = tcol <= trow
    eye = jnp.where(tcol == trow, 1.0, 0.0)

    ng = D // GROUP_LANES
    unit = [(bb, slice(g * GROUP_LANES, (g + 1) * GROUP_LANES)) for bb in range(nb) for g in range(ng)]
    groups = range(len(unit))
    bdm32 = bdm.astype(F32)
    at = [at_all[bb][:, s] for bb, s in unit]
    rt = [rt_all[bb][:, s].astype(BF16) for bb, s in unit]
    v = [v_ref[bb, :, s] for bb, s in unit]
    gm = [_dot_nt(jnp.concatenate([at[g].astype(BF16), rt[g]], axis=0),
                  jnp.concatenate([bd(bt_all[unit[g][0]][:, unit[g][1]]),
                                   bd(kt_all[unit[g][0]][:, unit[g][1]])], axis=0)) for g in groups]
    m = [jnp.where(strict, gm[g][:L, :GROUP_LANES], 0.0) for g in groups]
    aak = [jnp.where(strict, gm[g][:L, GROUP_LANES:], 0.0).astype(BF16) for g in groups]
    arb = [jnp.where(incl, gm[g][L:, :GROUP_LANES], 0.0).astype(BF16) for g in groups]
    ark = [jnp.where(incl, gm[g][L:, GROUP_LANES:], 0.0).astype(BF16) for g in groups]
    zt = [z_ref[g] for g in groups]
    ztb = [zt[g].astype(BF16) for g in groups]
    bdv = [bd(v[g]) for g in groups]
    w1 = [_dot(aak[g], bdv[g]) for g in groups]
    y0 = [_dot_nt(rt[g], ztb[g]) + _dot(ark[g], bdv[g]) for g in groups]
    steps = L.bit_length() - 2
    x = [eye + m[g] for g in groups]
    m = [_dot(m[g].astype(BF16), bd(m[g])) for g in groups]
    for i in range(steps):
        mb = [m[g].astype(BF16) for g in groups]
        if i == steps - 1:
            x = [x[g] + _dot(mb[g], bd(x[g])) for g in groups]
        else:
            x, m = ([x[g] + _dot(mb[g], bd(x[g])) for g in groups],
                    [_dot(mb[g], bd(m[g])) for g in groups])
    xb = [x[g].astype(BF16) for g in groups]
    ap = [_dot(xb[g], bd(at[g])).astype(BF16) for g in groups]
    u0 = [_dot(xb[g], bd(w1[g])) for g in groups]
    u = [_dot_nt(ap[g], ztb[g]) + u0[g] for g in groups]
    y = [y0[g] + _dot(arb[g], bd(u[g])) for g in groups]
    upd = [_dot_tn(jnp.concatenate([u[g], v[g]], axis=0).astype(BF16),
                   jnp.concatenate([bh_all[unit[g][0]][:, unit[g][1]],
                                    kh_all[unit[g][0]][:, unit[g][1]]], axis=0).astype(BF16))
           for g in groups]
    for g in groups:
        bb, s = unit[g]
        y_ref[bb, :, s] = y[g]
        z_ref[g] = p_last[bb][:, s] * zt[g] + upd[g] * bdm32


def _rwkv_chunk(r, lw, k, v, al, be, chunk):
    B, T, D = r.shape
    i = lax.broadcasted_iota(jnp.int32, (chunk, chunk), 0)
    j = lax.broadcasted_iota(jnp.int32, (chunk, chunk), 1)
    consts = [(j <= i).astype(BF16), _seg_matrix()]
    nb = REC_BATCH if B % REC_BATCH == 0 else 1
    spec = pl.BlockSpec((nb, chunk, D), lambda b, t: (b, t, 0))
    return pl.pallas_call(
        _rwkv_chunk_kernel,
        grid=(B // nb, T // chunk),
        in_specs=[spec] * 6 + [_const_spec(c.shape) for c in consts],
        out_specs=spec,
        out_shape=jax.ShapeDtypeStruct((B, T, D), F32),
        scratch_shapes=[pltpu.VMEM((nb * (D // GROUP_LANES), GROUP_LANES, GROUP_LANES), F32)],
        compiler_params=_params(2),
        name="rwkv_chunk",
    )(r, lw, k, v, al, be, *consts)


def _rwkv_post_kernel(x_ref, y_ref, r_ref, k_ref, v_ref, g_ref, lng_ref, lnb_ref, rk_ref,
                      wo_ref, e_ref, o_ref):
    e = e_ref[...]
    y = y_ref[0]
    inv_n = 1.0 / RWKV_HEAD
    d = y - _seg_sum(y, e) * inv_n
    var = _seg_sum(d * d, e) * inv_n
    yn = d * lax.rsqrt(var + GN_EPS) * lng_ref[...] + lnb_ref[...]
    bonus = _seg_sum(r_ref[0] * k_ref[0] * rk_ref[...], e) * v_ref[0]
    out = ((yn + bonus) * g_ref[0]).astype(BF16)
    o_ref[0] = x_ref[0] + _dot(out, wo_ref[...])


def _rwkv_post(x, y, r, k, v, gate, ln_g, ln_b, r_k, w_o, tm):
    B, T, D = x.shape
    row = lambda p: p.reshape(1, D)
    consts = [row(ln_g), row(ln_b), row(r_k), w_o.astype(BF16), _seg_matrix()]
    return pl.pallas_call(
        _rwkv_post_kernel,
        grid=(B, T // tm),
        in_specs=[_tile_spec(tm, D)] * 6 + [_const_spec(c.shape) for c in consts],
        out_specs=_tile_spec(tm, D),
        out_shape=jax.ShapeDtypeStruct((B, T, D), F32),
        compiler_params=_params(2),
        name="rwkv_post",
    )(x, y, r, k, v, gate, *consts)


def _conformer_kernel(kw, tm, x_ref, g_ref, win_ref, bin_ref, dw_ref, dwb_ref, lng_ref, lnb_ref,
                      wout_ref, bout_ref, o_ref, cbuf, obuf):
    D = x_ref.shape[-1]
    hdr = cbuf.shape[0] - tm
    x = x_ref[0]
    h = _rms(x, g_ref[...]).astype(BF16)
    u = _dot(h, win_ref[...]) + bin_ref[...]
    glu = u[:, :D] * _sigmoid(u[:, D:])

    @pl.when(pl.program_id(1) == 0)
    def _():
        cbuf[0:hdr, :] = jnp.zeros((hdr, D), F32)

    cbuf[hdr:hdr + tm, :] = glu
    rb = 64
    cb = 128
    base0 = hdr - (kw - 1)
    for c in range(D // cb):
        cs = slice(c * cb, (c + 1) * cb)
        for r0 in range(0, tm, rb):
            acc = None
            for s in range(SUBLANES):
                rows = rb + (SUBLANES if s else 0)
                part = None
                for k in range(kw):
                    if (base0 + k) % SUBLANES != s:
                        continue
                    a0 = r0 + base0 + k - s
                    term = dw_ref[k:k + 1, cs] * cbuf[a0:a0 + rows, cs]
                    part = term if part is None else part + term
                if part is None:
                    continue
                part = part[s:s + rb]
                acc = part if acc is None else acc + part
            obuf[r0:r0 + rb, cs] = acc
    cbuf[0:hdr, :] = cbuf[tm:tm + hdr, :]

    c = obuf[...] + dwb_ref[...]
    mu = jnp.mean(c, axis=-1, keepdims=True)
    var = jnp.mean(jnp.square(c - mu), axis=-1, keepdims=True)
    z = (c - mu) * lax.rsqrt(var + LN_EPS) * lng_ref[...] + lnb_ref[...]
    z = (z * _sigmoid(z)).astype(BF16)
    o_ref[0] = x + (_dot(z, wout_ref[...]) + bout_ref[...])


def _conformer(x, g, w_in, b_in, dw, dw_b, ln_g, ln_b, w_out, b_out, tm):
    B, T, D = x.shape
    kw = dw.shape[0]
    hdr = -(-(kw - 1) // SUBLANES) * SUBLANES
    row = lambda p: p.reshape(1, -1)
    consts = [row(g), w_in.astype(BF16), row(b_in), dw, row(dw_b), row(ln_g), row(ln_b),
              w_out.astype(BF16), row(b_out)]
    return pl.pallas_call(
        functools.partial(_conformer_kernel, kw, tm),
        grid=(B, T // tm),
        in_specs=[_tile_spec(tm, D)] + [_const_spec(c.shape) for c in consts],
        out_specs=_tile_spec(tm, D),
        out_shape=jax.ShapeDtypeStruct((B, T, D), F32),
        scratch_shapes=[pltpu.VMEM((hdr + tm, D), F32), pltpu.VMEM((tm, D), F32)],
        compiler_params=_params(2),
        name="conformer",
    )(x, *consts)


def _kv_kernel(mem_ref, g_ref, w_ref, o_ref):
    memn = _rms(mem_ref[0], g_ref[...]).astype(BF16)
    o_ref[0, 0] = _dot(memn, w_ref[0]).astype(BF16)


def _kv_proj(mem, g, w_kv):
    B, M, D = mem.shape
    L = w_kv.shape[0]
    return pl.pallas_call(
        _kv_kernel,
        grid=(L, B),
        in_specs=[pl.BlockSpec((1, M, D), lambda l, b: (b, 0, 0)),
                  pl.BlockSpec((1, D), lambda l, b: (0, 0)),
                  pl.BlockSpec((1, D, 2 * D), lambda l, b: (l, 0, 0))],
        out_specs=pl.BlockSpec((1, 1, M, 2 * D), lambda l, b: (l, b, 0, 0)),
        out_shape=jax.ShapeDtypeStruct((L, B, M, 2 * D), BF16),
        compiler_params=_params(2),
        name="kv_proj",
    )(mem, g.reshape(1, D), w_kv.astype(BF16))


def _xattn_kernel(x_ref, g_ref, wq_ref, kv_ref, wo_ref, o_ref):
    D = x_ref.shape[-1]
    hd = D // XATTN_HEADS
    x = x_ref[0]
    h = _rms(x, g_ref[...]).astype(BF16)
    q = _dot(h, wq_ref[...]).astype(BF16)
    outs = []
    for i in range(XATTN_HEADS):
        km = kv_ref[0, 0, :, i * hd:(i + 1) * hd]
        vm = kv_ref[0, 0, :, D + i * hd:D + (i + 1) * hd]
        s = lax.dot_general(q[:, i * hd:(i + 1) * hd], km, (((1,), (1,)), ((), ())),
                            preferred_element_type=F32) * (hd ** -0.5)
        s = jnp.exp(s - jnp.max(s, axis=-1, keepdims=True))
        p = (s / jnp.sum(s, axis=-1, keepdims=True)).astype(BF16)
        outs.append(_dot(p, vm).astype(BF16))
    o = jnp.concatenate(outs, axis=-1)
    o_ref[0] = x + _dot(o, wo_ref[...])


def _xattn(x, g, w_q, kv, layer, w_o, tm):
    B, T, D = x.shape
    M = kv.shape[2]
    consts = [g.reshape(1, D), w_q.astype(BF16)]
    return pl.pallas_call(
        _xattn_kernel,
        grid=(B, T // tm),
        in_specs=[_tile_spec(tm, D)] + [_const_spec(c.shape) for c in consts]
        + [pl.BlockSpec((1, 1, M, 2 * D), lambda b, t: (layer, b, 0, 0)), _const_spec((D, D))],
        out_specs=_tile_spec(tm, D),
        out_shape=jax.ShapeDtypeStruct((B, T, D), F32),
        compiler_params=_params(2),
        name="xattn",
    )(x, *consts, kv, w_o.astype(BF16))


def _ffn_kernel(kw, cw, rb, has_final, *refs):
    if has_final:
        (x_ref, g_ref, win_ref, dw_ref, wout_ref, fg_ref, o_ref,
         carry, h_ref, ug_ref, uv_ref, act_ref) = refs
    else:
        (x_ref, g_ref, win_ref, dw_ref, wout_ref, o_ref,
         carry, h_ref, ug_ref, uv_ref, act_ref) = refs
    tm = x_ref.shape[1]
    F = wout_ref.shape[0]
    hdr = SUBLANES
    h_ref[...] = _rms(x_ref[0], g_ref[...]).astype(BF16)

    @pl.when(pl.program_id(1) == 0)
    def _():
        carry[...] = jnp.zeros_like(carry)

    def proj(slot, c0):
        for u_ref, cc in ((ug_ref, c0), (uv_ref, F + c0)):
            u_ref[slot, 0:hdr, :] = carry[:, cc:cc + cw]
            u_ref[slot, hdr:hdr + tm, :] = _dot(h_ref[...], win_ref[:, cc:cc + cw])
            carry[:, cc:cc + cw] = u_ref[slot, tm:tm + hdr, :]

    def conv_rows(u_ref, slot, r0, cc):
        blk = u_ref[slot, r0:r0 + rb + hdr, :]
        out = dw_ref[kw - 1:kw, cc:cc + cw] * blk[hdr:, :]
        for k in range(kw - 1):
            off = hdr - (kw - 1) + k
            out = out + dw_ref[k:k + 1, cc:cc + cw] * blk[off:off + rb, :]
        return out

    starts = list(range(0, F, cw))
    proj(0, starts[0])
    for i, c0 in enumerate(starts):
        slot = i % 2
        if i + 1 < len(starts):
            proj(1 - slot, starts[i + 1])
        for r0 in range(0, tm, rb):
            gate = conv_rows(ug_ref, slot, r0, c0)
            val = conv_rows(uv_ref, slot, r0, F + c0)
            act_ref[r0:r0 + rb, c0:c0 + cw] = (gate * _sigmoid(gate) * val).astype(BF16)
    y = x_ref[0] + _dot(act_ref[...], wout_ref[...])
    if has_final:
        y = _rms(y, fg_ref[...])
    o_ref[0] = y


def _ffn(x, g, w_in, dw, w_out, final_g, tm, cw):
    B, T, D = x.shape
    kw = dw.shape[0]
    F = w_out.shape[0]
    has_final = final_g is not None
    consts = [g.reshape(1, D), w_in.astype(BF16), dw, w_out.astype(BF16)]
    if has_final:
        consts.append(final_g.reshape(1, D))
    return pl.pallas_call(
        functools.partial(_ffn_kernel, kw, cw, FFN_RB, has_final),
        grid=(B, T // tm),
        in_specs=[_tile_spec(tm, D)] + [_const_spec(c.shape) for c in consts],
        out_specs=_tile_spec(tm, D),
        out_shape=jax.ShapeDtypeStruct((B, T, D), F32),
        scratch_shapes=[pltpu.VMEM((SUBLANES, 2 * F), F32),
                        pltpu.VMEM((tm, D), BF16),
                        pltpu.VMEM((2, SUBLANES + tm, cw), F32),
                        pltpu.VMEM((2, SUBLANES + tm, cw), F32),
                        pltpu.VMEM((tm, F), BF16)],
        compiler_params=_params(2),
        name="ffn",
    )(x, *consts)


TM = 256
TM_WIDE = 512
REC_CHUNK = 64
REC_BATCH = 2
FFN_CW = 256
FFN_RB = 64


def rwkv_layer(x, g, mu, w_r, w_k, w_v, w_o, w0, w1, w2, a0, a1, a2, g1, g2, k_k, k_a, r_k,
               ln_g, ln_b, v_first, v_res, tm, tb):
    r, w, k, v, al, be, gate = _rwkv_pre(x, g, mu, w_r, w_k, w_v, w0, w1, w2, a0, a1, a2, g1, g2,
                                         k_k, k_a, v_first, v_res, tm)
    y = _rwkv_chunk(r, w, k, v, al, be, tb)
    x = _rwkv_post(x, y, r, k, v, gate, ln_g, ln_b, r_k.reshape(-1), w_o, tm)
    return x, v


def kernel(x, mem, mem_norm_g, norm_mix_g, norm_xattn_g, norm_ffn_g, final_norm_g, rwkv_mu, rwkv_w_r, rwkv_w_k, rwkv_w_v, rwkv_w_o, rwkv_w0, rwkv_w1, rwkv_w2, rwkv_a0, rwkv_a1, rwkv_a2, rwkv_g1, rwkv_g2, rwkv_k_k, rwkv_k_a, rwkv_r_k, rwkv_ln_g, rwkv_ln_b, rwkv_v0, rwkv_v1, rwkv_v2, conv_w_in, conv_b_in, conv_dw, conv_dw_b, conv_ln_g, conv_ln_b, conv_w_out, conv_b_out, xattn_w_q, xattn_w_kv, xattn_w_o, ffn_w_in, ffn_dw, ffn_w_out):
    B, T, D = x.shape
    depth = norm_mix_g.shape[0]
    tm = min(TM, T)
    tmw = min(TM_WIDE, T)
    kv = _kv_proj(mem, mem_norm_g, xattn_w_kv)
    v_first = None
    ia = 0
    ib = 0
    for layer in range(depth):
        if layer % 2 == 0:
            v_res = None if ia == 0 else (rwkv_v0[ia - 1], rwkv_v1[ia - 1], rwkv_v2[ia - 1])
            x, v = rwkv_layer(
                x, norm_mix_g[layer], rwkv_mu[ia], rwkv_w_r[ia], rwkv_w_k[ia], rwkv_w_v[ia],
                rwkv_w_o[ia], rwkv_w0[ia], rwkv_w1[ia], rwkv_w2[ia], rwkv_a0[ia], rwkv_a1[ia],
                rwkv_a2[ia], rwkv_g1[ia], rwkv_g2[ia], rwkv_k_k[ia], rwkv_k_a[ia], rwkv_r_k[ia],
                rwkv_ln_g[ia], rwkv_ln_b[ia], v_first, v_res, tm, min(REC_CHUNK, T))
            if ia == 0:
                v_first = v
            ia += 1
        else:
            x = _conformer(x, norm_mix_g[layer], conv_w_in[ib], conv_b_in[ib], conv_dw[ib],
                           conv_dw_b[ib], conv_ln_g[ib], conv_ln_b[ib], conv_w_out[ib],
                           conv_b_out[ib], tm)
            ib += 1
        x = _xattn(x, norm_xattn_g[layer], xattn_w_q[layer], kv, layer, xattn_w_o[layer], tmw)
        x = _ffn(x, norm_ffn_g[layer], ffn_w_in[layer], ffn_dw[layer], ffn_w_out[layer],
                 final_norm_g if layer == depth - 1 else None, tmw, FFN_CW)
    return x
```

```python
import functools

import jax
import jax.numpy as jnp
from jax import lax
from jax.experimental import pallas as pl
from jax.experimental.pallas import tpu as pltpu

F32 = jnp.float32
BF16 = jnp.bfloat16

RWKV_HEAD = 64
GN_EPS = 64e-5
NORM_EPS = 1e-6
LN_EPS = 1e-5
XATTN_HEADS = 4
SUBLANES = 8
SEG_BLOCK = 256
GROUP_LANES = 128
VMEM_LIMIT = 56 * 1024 * 1024


def _params(n_grid):
    return pltpu.CompilerParams(
        dimension_semantics=("arbitrary",) * n_grid, vmem_limit_bytes=VMEM_LIMIT)


def _dot(a, b):
    return jnp.dot(a, b, preferred_element_type=F32)


def _rms(x, g):
    return x * lax.rsqrt(jnp.mean(x * x, axis=-1, keepdims=True) + NORM_EPS) * g


def _sigmoid(x):
    return 1.0 / (1.0 + jnp.exp(-x))


def _softplus(x):
    return jnp.maximum(x, 0.0) + jnp.log(1.0 + jnp.exp(-jnp.abs(x)))


def _seg_sum(y, e):
    outs = []
    for c in range(y.shape[-1] // SEG_BLOCK):
        yb = y[:, c * SEG_BLOCK:(c + 1) * SEG_BLOCK]
        hi = yb.astype(BF16)
        lo = (yb - hi.astype(F32)).astype(BF16)
        outs.append(_dot(hi, e) + _dot(lo, e))
    return jnp.concatenate(outs, axis=-1)


def _seg_matrix():
    i = lax.broadcasted_iota(jnp.int32, (SEG_BLOCK, SEG_BLOCK), 0) // RWKV_HEAD
    j = lax.broadcasted_iota(jnp.int32, (SEG_BLOCK, SEG_BLOCK), 1) // RWKV_HEAD
    return (i == j).astype(BF16)


def _const_spec(shape):
    zeros = (0,) * len(shape)
    return pl.BlockSpec(shape, lambda *_: zeros, pipeline_mode=pl.Buffered(1))


def _tile_spec(tm, d):
    return pl.BlockSpec((1, tm, d), lambda b, t: (b, t, 0))


def _prev_rows_spec(tm, d):
    per = tm // SUBLANES
    return pl.BlockSpec((1, SUBLANES, d), lambda b, t: (b, jnp.maximum(t * per - 1, 0), 0))


def _rwkv_pre_kernel(has_vres, *refs):
    if has_vres:
        (x_ref, xp_ref, g_ref, mu_ref, wr_ref, wk_ref, wv_ref, w0_ref, w1_ref, w2_ref,
         a0_ref, a1_ref, a2_ref, g1_ref, g2_ref, kk_ref, ka_ref, rk_ref, e_ref,
         vf_ref, v0_ref, v1_ref, v2_ref,
         r_out, w_out, k_out, v_out, al_out, be_out, g_out, bo_out) = refs
    else:
        (x_ref, xp_ref, g_ref, mu_ref, wr_ref, wk_ref, wv_ref, w0_ref, w1_ref, w2_ref,
         a0_ref, a1_ref, a2_ref, g1_ref, g2_ref, kk_ref, ka_ref, rk_ref, e_ref,
         r_out, w_out, k_out, v_out, al_out, be_out, g_out, bo_out) = refs
    t = pl.program_id(1)
    g = g_ref[...]
    h = _rms(x_ref[0], g)
    hp = _rms(xp_ref[0], g)
    prev = jnp.where(t > 0, hp[SUBLANES - 1:SUBLANES, :], 0.0)
    row = lax.broadcasted_iota(jnp.int32, h.shape, 0)
    hs = jnp.where(row == 0, prev, pltpu.roll(h, 1, axis=0))
    xx = hs - h

    def mix(i):
        return (h + xx * mu_ref[i:i + 1, :]).astype(BF16)

    e = e_ref[...]
    r = _dot(mix(0), wr_ref[...])
    wl = _dot(jnp.tanh(_dot(mix(1), w1_ref[...])).astype(BF16), w2_ref[...])
    k = _dot(mix(2), wk_ref[...])
    xv = mix(3)
    v = _dot(xv, wv_ref[...])
    al = _dot(_dot(mix(4), a1_ref[...]).astype(BF16), a2_ref[...])
    gate = _dot(_sigmoid(_dot(mix(5), g1_ref[...])).astype(BF16), g2_ref[...])

    logw = -_softplus(-(w0_ref[...] + wl)) - 0.5
    log_decay = -jnp.exp(logw)
    a = _sigmoid(a0_ref[...] + al)
    kk = k * kk_ref[...]
    kk = kk / jnp.maximum(jnp.sqrt(_seg_sum(kk * kk, e)), 1e-12)
    k = k * (1.0 + (a - 1.0) * ka_ref[...])
    if has_vres:
        vl = _dot(_dot(xv, v1_ref[...]).astype(BF16), v2_ref[...])
        v = v + (vf_ref[0] - v) * _sigmoid(v0_ref[...] + vl)
    r_out[0] = r
    w_out[0] = log_decay
    k_out[0] = k
    v_out[0] = v
    al_out[0] = -kk
    be_out[0] = kk * a
    g_out[0] = gate
    bo_out[0] = _seg_sum(r * k * rk_ref[...], e) * v


def _rwkv_pre(x, g, mu, wr, wk, wv, w0, w1, w2, a0, a1, a2, g1, g2, k_k, k_a, r_k, v_first, v_res, tm):
    B, T, D = x.shape
    has_vres = v_res is not None
    row = lambda p: p.reshape(1, D)
    bf = lambda p: p.astype(BF16)
    args = [x, x, row(g), mu, bf(wr), bf(wk), bf(wv), row(w0), bf(w1), bf(w2),
            row(a0), bf(a1), bf(a2), bf(g1), bf(g2), row(k_k), row(k_a), row(r_k), _seg_matrix()]
    specs = [_tile_spec(tm, D), _prev_rows_spec(tm, D)] + [_const_spec(a.shape) for a in args[2:]]
    if has_vres:
        v0, v1, v2 = v_res
        extra = [v_first, row(v0), bf(v1), bf(v2)]
        args += extra
        specs += [_tile_spec(tm, D)] + [_const_spec(a.shape) for a in extra[1:]]
    out = jax.ShapeDtypeStruct((B, T, D), F32)
    return pl.pallas_call(
        functools.partial(_rwkv_pre_kernel, has_vres),
        grid=(B, T // tm),
        in_specs=specs,
        out_specs=[_tile_spec(tm, D)] * 8,
        out_shape=[out] * 8,
        compiler_params=_params(2),
        name="rwkv_pre",
    )(*args)


def _dot_nt(a, b):
    return lax.dot_general(a, b, (((1,), (1,)), ((), ())), preferred_element_type=F32)


def _dot_tn(a, b):
    return lax.dot_general(a, b, (((0,), (0,)), ((), ())), preferred_element_type=F32)


def _rwkv_chunk_kernel(r_ref, lw_ref, k_ref, v_ref, al_ref, be_ref, tril_ref, bdm_ref, y_ref, z_ref):
    nb, L, D = r_ref.shape
    reps = GROUP_LANES // L

    @pl.when(pl.program_id(1) == 0)
    def _():
        z_ref[...] = jnp.zeros_like(z_ref)

    tril = tril_ref[...]
    bdm = bdm_ref[...]

    def bd(side):
        return jnp.concatenate([side.astype(BF16)] * reps, axis=0) * bdm

    trow = lax.broadcasted_iota(jnp.int32, (L, GROUP_LANES), 0)
    tcol = lax.broadcasted_iota(jnp.int32, (L, GROUP_LANES), 1) % L
    strict = tcol < trow
    incl = tcol <= trow
    eye = jnp.where(tcol == trow, 1.0, 0.0)
    bdm32 = bdm.astype(F32)
    ng = D // GROUP_LANES
    unit = [(bb, slice(g * GROUP_LANES, (g + 1) * GROUP_LANES)) for bb in range(nb) for g in range(ng)]
    groups = range(len(unit))

    at_all, rt_all, bt_all, kt_all, bh_all, kh_all, p_all = [], [], [], [], [], [], []
    for bb in range(nb):
        lw = lw_ref[bb]
        hi = lw.astype(BF16)
        rest = lw - hi.astype(F32)
        mid = rest.astype(BF16)
        lo = (rest - mid.astype(F32)).astype(BF16)
        cum = _dot(tril, hi) + _dot(tril, mid) + _dot(tril, lo)
        cum_last = cum[L - 1:L, :]
        e_neg = jnp.exp(-cum)
        e_end = jnp.exp(cum_last - cum)
        at_all.append((al_ref[bb] * jnp.exp(cum - lw)).astype(BF16))
        rt_all.append((r_ref[bb] * jnp.exp(cum)).astype(BF16))
        bt_all.append((be_ref[bb] * e_neg).astype(BF16))
        kt_all.append((k_ref[bb] * e_neg).astype(BF16))
        bh_all.append((be_ref[bb] * e_end).astype(BF16))
        kh_all.append((k_ref[bb] * e_end).astype(BF16))
        p_all.append(jnp.exp(cum_last))

    at, rt, bt, kt, bh, kh, p_last = ([a[bb][:, s] for bb, s in unit]
                                      for a in (at_all, rt_all, bt_all, kt_all, bh_all, kh_all, p_all))
    v = [v_ref[bb, :, s] for bb, s in unit]
    gm = [_dot_nt(jnp.concatenate([at[g], rt[g]], axis=0),
                  jnp.concatenate([bd(bt[g]), bd(kt[g])], axis=0)) for g in groups]
    m = [jnp.where(strict, gm[g][:L, :GROUP_LANES], 0.0) for g in groups]
    aak = [jnp.where(strict, gm[g][:L, GROUP_LANES:], 0.0).astype(BF16) for g in groups]
    arb = [jnp.where(incl, gm[g][L:, :GROUP_LANES], 0.0).astype(BF16) for g in groups]
    ark = [jnp.where(incl, gm[g][L:, GROUP_LANES:], 0.0).astype(BF16) for g in groups]
    zt = [z_ref[g] for g in groups]
    ztb = [zt[g].astype(BF16) for g in groups]
    bdv = [bd(v[g]) for g in groups]
    w1 = [_dot_nt(at[g], ztb[g]) + _dot(aak[g], bdv[g]) for g in groups]
    y0 = [_dot_nt(rt[g], ztb[g]) + _dot(ark[g], bdv[g]) for g in groups]
    steps = L.bit_length() - 2
    x = [eye + m[g] for g in groups]
    m = [_dot(m[g].astype(BF16), bd(m[g])) for g in groups]
    for i in range(steps):
        mb = [m[g].astype(BF16) for g in groups]
        if i == steps - 1:
            x = [x[g] + _dot(mb[g], bd(x[g])) for g in groups]
        else:
            x, m = ([x[g] + _dot(mb[g], bd(x[g])) for g in groups],
                    [_dot(mb[g], bd(m[g])) for g in groups])
    xb = [x[g].astype(BF16) for g in groups]
    u = [_dot(xb[g], bd(w1[g])) for g in groups]
    y = [y0[g] + _dot(arb[g], bd(u[g])) for g in groups]
    upd = [_dot_tn(jnp.concatenate([u[g], v[g]], axis=0).astype(BF16),
                   jnp.concatenate([bh[g], kh[g]], axis=0)) for g in groups]
    for g in groups:
        bb, s = unit[g]
        y_ref[bb, :, s] = y[g]
        z_ref[g] = p_last[g] * zt[g] + upd[g] * bdm32


def _rwkv_chunk(r, lw, k, v, al, be, chunk):
    B, T, D = r.shape
    i = lax.broadcasted_iota(jnp.int32, (chunk, chunk), 0)
    j = lax.broadcasted_iota(jnp.int32, (chunk, chunk), 1)
    consts = [(j <= i).astype(BF16), _seg_matrix()[:GROUP_LANES, :GROUP_LANES]]
    nb = REC_BATCH if B % REC_BATCH == 0 else 1
    spec = pl.BlockSpec((nb, chunk, D), lambda b, t: (b, t, 0))
    return pl.pallas_call(
        _rwkv_chunk_kernel,
        grid=(B // nb, T // chunk),
        in_specs=[spec] * 6 + [_const_spec(c.shape) for c in consts],
        out_specs=spec,
        out_shape=jax.ShapeDtypeStruct((B, T, D), F32),
        scratch_shapes=[pltpu.VMEM((nb * (D // GROUP_LANES), GROUP_LANES, GROUP_LANES), F32)],
        compiler_params=_params(2),
        name="rwkv_chunk",
    )(r, lw, k, v, al, be, *consts)


def _rwkv_post_kernel(x_ref, y_ref, bo_ref, g_ref, lng_ref, lnb_ref, wo_ref, e_ref, o_ref):
    e = e_ref[...]
    y = y_ref[0]
    inv_n = 1.0 / RWKV_HEAD
    d = y - _seg_sum(y, e) * inv_n
    var = _seg_sum(d * d, e) * inv_n
    yn = d * lax.rsqrt(var + GN_EPS) * lng_ref[...] + lnb_ref[...]
    out = ((yn + bo_ref[0]) * g_ref[0]).astype(BF16)
    o_ref[0] = x_ref[0] + _dot(out, wo_ref[...])


def _rwkv_post(x, y, bonus, gate, ln_g, ln_b, w_o, tm):
    B, T, D = x.shape
    row = lambda p: p.reshape(1, D)
    consts = [row(ln_g), row(ln_b), w_o.astype(BF16), _seg_matrix()]
    return pl.pallas_call(
        _rwkv_post_kernel,
        grid=(B, T // tm),
        in_specs=[_tile_spec(tm, D)] * 4 + [_const_spec(c.shape) for c in consts],
        out_specs=_tile_spec(tm, D),
        out_shape=jax.ShapeDtypeStruct((B, T, D), F32),
        compiler_params=_params(2),
        name="rwkv_post",
    )(x, y, bonus, gate, *consts)


def _conformer_kernel(kw, tm, x_ref, g_ref, win_ref, bin_ref, dw_ref, dwb_ref, lng_ref, lnb_ref,
                      wout_ref, bout_ref, o_ref, cbuf, obuf):
    D = x_ref.shape[-1]
    hdr = cbuf.shape[0] - tm
    x = x_ref[0]
    h = _rms(x, g_ref[...]).astype(BF16)
    u = _dot(h, win_ref[...]) + bin_ref[...]
    glu = u[:, :D] * _sigmoid(u[:, D:])

    @pl.when(pl.program_id(1) == 0)
    def _():
        cbuf[0:hdr, :] = jnp.zeros((hdr, D), F32)

    cbuf[hdr:hdr + tm, :] = glu
    rb = min(128, tm)
    cb = 128
    base0 = hdr - (kw - 1)
    for c in range(D // cb):
        cs = slice(c * cb, (c + 1) * cb)
        for r0 in range(0, tm, rb):
            acc = None
            for s in range(SUBLANES):
                rows = rb + (SUBLANES if s else 0)
                part = None
                for k in range(kw):
                    if (base0 + k) % SUBLANES != s:
                        continue
                    a0 = r0 + base0 + k - s
                    term = dw_ref[k:k + 1, cs] * cbuf[a0:a0 + rows, cs]
                    part = term if part is None else part + term
                if part is None:
                    continue
                part = part[s:s + rb]
                acc = part if acc is None else acc + part
            obuf[r0:r0 + rb, cs] = acc
    cbuf[0:hdr, :] = cbuf[tm:tm + hdr, :]

    c = obuf[...] + dwb_ref[...]
    mu = jnp.mean(c, axis=-1, keepdims=True)
    var = jnp.mean(jnp.square(c - mu), axis=-1, keepdims=True)
    z = (c - mu) * lax.rsqrt(var + LN_EPS) * lng_ref[...] + lnb_ref[...]
    z = (z * _sigmoid(z)).astype(BF16)
    o_ref[0] = x + (_dot(z, wout_ref[...]) + bout_ref[...])


def _conformer(x, g, w_in, b_in, dw, dw_b, ln_g, ln_b, w_out, b_out, tm):
    B, T, D = x.shape
    kw = dw.shape[0]
    hdr = -(-(kw - 1) // SUBLANES) * SUBLANES
    row = lambda p: p.reshape(1, -1)
    consts = [row(g), w_in.astype(BF16), row(b_in), dw, row(dw_b), row(ln_g), row(ln_b),
              w_out.astype(BF16), row(b_out)]
    return pl.pallas_call(
        functools.partial(_conformer_kernel, kw, tm),
        grid=(B, T // tm),
        in_specs=[_tile_spec(tm, D)] + [_const_spec(c.shape) for c in consts],
        out_specs=_tile_spec(tm, D),
        out_shape=jax.ShapeDtypeStruct((B, T, D), F32),
        scratch_shapes=[pltpu.VMEM((hdr + tm, D), F32), pltpu.VMEM((tm, D), F32)],
        compiler_params=_params(2),
        name="conformer",
    )(x, *consts)


def _kv_kernel(mem_ref, g_ref, w_ref, o_ref):
    memn = _rms(mem_ref[0], g_ref[...]).astype(BF16)
    o_ref[0, 0] = _dot(memn, w_ref[0]).astype(BF16)


def _kv_proj(mem, g, w_kv):
    B, M, D = mem.shape
    L = w_kv.shape[0]
    return pl.pallas_call(
        _kv_kernel,
        grid=(L, B),
        in_specs=[pl.BlockSpec((1, M, D), lambda l, b: (b, 0, 0)),
                  pl.BlockSpec((1, D), lambda l, b: (0, 0)),
                  pl.BlockSpec((1, D, 2 * D), lambda l, b: (l, 0, 0))],
        out_specs=pl.BlockSpec((1, 1, M, 2 * D), lambda l, b: (l, b, 0, 0)),
        out_shape=jax.ShapeDtypeStruct((L, B, M, 2 * D), BF16),
        compiler_params=_params(2),
        name="kv_proj",
    )(mem, g.reshape(1, D), w_kv.astype(BF16))


def _xattn_kernel(x_ref, g_ref, wq_ref, kv_ref, wo_ref, o_ref):
    D = x_ref.shape[-1]
    hd = D // XATTN_HEADS
    x = x_ref[0]
    h = _rms(x, g_ref[...]).astype(BF16)
    q = _dot(h, wq_ref[...]).astype(BF16)
    heads = range(XATTN_HEADS)
    s = [_dot_nt(q[:, i * hd:(i + 1) * hd], kv_ref[0, 0, :, i * hd:(i + 1) * hd]) * (hd ** -0.5)
         for i in heads]
    e = [jnp.exp(s[i] - jnp.max(s[i], axis=-1, keepdims=True)) for i in heads]
    p = [(e[i] * (1.0 / jnp.sum(e[i], axis=-1, keepdims=True))).astype(BF16) for i in heads]
    outs = [_dot(p[i], kv_ref[0, 0, :, D + i * hd:D + (i + 1) * hd]).astype(BF16) for i in heads]
    o = jnp.concatenate(outs, axis=-1)
    o_ref[0] = x + _dot(o, wo_ref[...])


def _xattn(x, g, w_q, kv, layer, w_o, tm):
    B, T, D = x.shape
    M = kv.shape[2]
    consts = [g.reshape(1, D), w_q.astype(BF16)]
    return pl.pallas_call(
        _xattn_kernel,
        grid=(B, T // tm),
        in_specs=[_tile_spec(tm, D)] + [_const_spec(c.shape) for c in consts]
        + [pl.BlockSpec((1, 1, M, 2 * D), lambda b, t: (layer, b, 0, 0)), _const_spec((D, D))],
        out_specs=_tile_spec(tm, D),
        out_shape=jax.ShapeDtypeStruct((B, T, D), F32),
        compiler_params=_params(2),
        name="xattn",
    )(x, *consts, kv, w_o.astype(BF16))


def _ffn_kernel(kw, cw, rb, has_final, *refs):
    if has_final:
        (x_ref, g_ref, win_ref, dw_ref, wout_ref, fg_ref, o_ref,
         carry, h_ref, ug_ref, uv_ref, act_ref) = refs
    else:
        (x_ref, g_ref, win_ref, dw_ref, wout_ref, o_ref,
         carry, h_ref, ug_ref, uv_ref, act_ref) = refs
    tm = x_ref.shape[1]
    F = wout_ref.shape[0]
    hdr = SUBLANES
    h_ref[...] = _rms(x_ref[0], g_ref[...]).astype(BF16)

    @pl.when(pl.program_id(1) == 0)
    def _():
        carry[...] = jnp.zeros_like(carry)

    def proj(slot, c0):
        for u_ref, cc in ((ug_ref, c0), (uv_ref, F + c0)):
            u_ref[slot, 0:hdr, :] = carry[:, cc:cc + cw]
            u_ref[slot, hdr:hdr + tm, :] = _dot(h_ref[...], win_ref[:, cc:cc + cw])
            carry[:, cc:cc + cw] = u_ref[slot, tm:tm + hdr, :]

    def conv_rows(u_ref, slot, r0, cc):
        blk = u_ref[slot, r0:r0 + rb + hdr, :]
        out = dw_ref[kw - 1:kw, cc:cc + cw] * blk[hdr:, :]
        for k in range(kw - 1):
            off = hdr - (kw - 1) + k
            out = out + dw_ref[k:k + 1, cc:cc + cw] * blk[off:off + rb, :]
        return out

    starts = list(range(0, F, cw))
    proj(0, starts[0])
    for i, c0 in enumerate(starts):
        slot = i % 2
        if i + 1 < len(starts):
            proj(1 - slot, starts[i + 1])
        for r0 in range(0, tm, rb):
            gate = conv_rows(ug_ref, slot, r0, c0)
            val = conv_rows(uv_ref, slot, r0, F + c0)
            act_ref[r0:r0 + rb, c0:c0 + cw] = (gate * _sigmoid(gate) * val).astype(BF16)
    y = x_ref[0] + _dot(act_ref[...], wout_ref[...])
    if has_final:
        y = _rms(y, fg_ref[...])
    o_ref[0] = y


def _ffn(x, g, w_in, dw, w_out, final_g, tm, cw):
    B, T, D = x.shape
    kw = dw.shape[0]
    F = w_out.shape[0]
    has_final = final_g is not None
    consts = [g.reshape(1, D), w_in.astype(BF16), dw, w_out.astype(BF16)]
    if has_final:
        consts.append(final_g.reshape(1, D))
    return pl.pallas_call(
        functools.partial(_ffn_kernel, kw, cw, FFN_RB, has_final),
        grid=(B, T // tm),
        in_specs=[_tile_spec(tm, D)] + [_const_spec(c.shape) for c in consts],
        out_specs=_tile_spec(tm, D),
        out_shape=jax.ShapeDtypeStruct((B, T, D), F32),
        scratch_shapes=[pltpu.VMEM((SUBLANES, 2 * F), F32),
                        pltpu.VMEM((tm, D), BF16),
                        pltpu.VMEM((2, SUBLANES + tm, cw), F32),
                        pltpu.VMEM((2, SUBLANES + tm, cw), F32),
                        pltpu.VMEM((tm, F), BF16)],
        compiler_params=_params(2),
        name="ffn",
    )(x, *consts)


TM = 256
TM_WIDE = 512
REC_CHUNK = 64
REC_BATCH = 2
FFN_CW = 256
FFN_RB = 64


def rwkv_layer(x, g, mu, w_r, w_k, w_v, w_o, w0, w1, w2, a0, a1, a2, g1, g2, k_k, k_a, r_k,
               ln_g, ln_b, v_first, v_res, tm, tb):
    r, w, k, v, al, be, gate, bonus = _rwkv_pre(x, g, mu, w_r, w_k, w_v, w0, w1, w2, a0, a1, a2, g1, g2,
                                                k_k, k_a, r_k.reshape(-1), v_first, v_res, tm)
    y = _rwkv_chunk(r, w, k, v, al, be, tb)
    x = _rwkv_post(x, y, bonus, gate, ln_g, ln_b, w_o, tm)
    return x, v


def kernel(x, mem, mem_norm_g, norm_mix_g, norm_xattn_g, norm_ffn_g, final_norm_g, rwkv_mu, rwkv_w_r, rwkv_w_k, rwkv_w_v, rwkv_w_o, rwkv_w0, rwkv_w1, rwkv_w2, rwkv_a0, rwkv_a1, rwkv_a2, rwkv_g1, rwkv_g2, rwkv_k_k, rwkv_k_a, rwkv_r_k, rwkv_ln_g, rwkv_ln_b, rwkv_v0, rwkv_v1, rwkv_v2, conv_w_in, conv_b_in, conv_dw, conv_dw_b, conv_ln_g, conv_ln_b, conv_w_out, conv_b_out, xattn_w_q, xattn_w_kv, xattn_w_o, ffn_w_in, ffn_dw, ffn_w_out):
    B, T, D = x.shape
    depth = norm_mix_g.shape[0]
    tm = min(TM, T)
    tmw = min(TM_WIDE, T)
    kv = _kv_proj(mem, mem_norm_g, xattn_w_kv)
    v_first = None
    ia = 0
    ib = 0
    for layer in range(depth):
        if layer % 2 == 0:
            v_res = None if ia == 0 else (rwkv_v0[ia - 1], rwkv_v1[ia - 1], rwkv_v2[ia - 1])
            x, v = rwkv_layer(
                x, norm_mix_g[layer], rwkv_mu[ia], rwkv_w_r[ia], rwkv_w_k[ia], rwkv_w_v[ia],
                rwkv_w_o[ia], rwkv_w0[ia], rwkv_w1[ia], rwkv_w2[ia], rwkv_a0[ia], rwkv_a1[ia],
                rwkv_a2[ia], rwkv_g1[ia], rwkv_g2[ia], rwkv_k_k[ia], rwkv_k_a[ia], rwkv_r_k[ia],
                rwkv_ln_g[ia], rwkv_ln_b[ia], v_first, v_res, tm, min(REC_CHUNK, T))
            if ia == 0:
                v_first = v
            ia += 1
        else:
            x = _conformer(x, norm_mix_g[layer], conv_w_in[ib], conv_b_in[ib], conv_dw[ib],
                           conv_dw_b[ib], conv_ln_g[ib], conv_ln_b[ib], conv_w_out[ib],
                           conv_b_out[ib], tm)
            ib += 1
        x = _xattn(x, norm_xattn_g[layer], xattn_w_q[layer], kv, layer, xattn_w_o[layer], tmw)
        x = _ffn(x, norm_ffn_g[layer], ffn_w_in[layer], ffn_dw[layer], ffn_w_out[layer],
                 final_norm_g if layer == depth - 1 else None, tmw, FFN_CW)
    return x
```

```python
import functools

import jax
import jax.numpy as jnp
from jax import lax
from jax.experimental import pallas as pl
from jax.experimental.pallas import tpu as pltpu

F32 = jnp.float32
BF16 = jnp.bfloat16

RWKV_HEAD = 64
GN_EPS = 64e-5
NORM_EPS = 1e-6
LN_EPS = 1e-5
XATTN_HEADS = 4
SUBLANES = 8
SEG_BLOCK = 256
GROUP_LANES = 128
VMEM_LIMIT = 56 * 1024 * 1024


def _params(n_grid):
    return pltpu.CompilerParams(
        dimension_semantics=("arbitrary",) * n_grid, vmem_limit_bytes=VMEM_LIMIT)


def _dot(a, b):
    return jnp.dot(a, b, preferred_element_type=F32)


def _rms(x, g):
    return x * lax.rsqrt(jnp.mean(x * x, axis=-1, keepdims=True) + NORM_EPS) * g


def _sigmoid(x):
    return 1.0 / (1.0 + jnp.exp(-x))


def _softplus(x):
    return jnp.maximum(x, 0.0) + jnp.log(1.0 + jnp.exp(-jnp.abs(x)))


def _seg_sum(y, e):
    outs = []
    for c in range(y.shape[-1] // SEG_BLOCK):
        yb = y[:, c * SEG_BLOCK:(c + 1) * SEG_BLOCK]
        hi = yb.astype(BF16)
        lo = (yb - hi.astype(F32)).astype(BF16)
        outs.append(_dot(hi, e) + _dot(lo, e))
    return jnp.concatenate(outs, axis=-1)


def _seg_matrix():
    i = lax.broadcasted_iota(jnp.int32, (SEG_BLOCK, SEG_BLOCK), 0) // RWKV_HEAD
    j = lax.broadcasted_iota(jnp.int32, (SEG_BLOCK, SEG_BLOCK), 1) // RWKV_HEAD
    return (i == j).astype(BF16)


def _const_spec(shape):
    zeros = (0,) * len(shape)
    return pl.BlockSpec(shape, lambda *_: zeros, pipeline_mode=pl.Buffered(1))


def _tile_spec(tm, d):
    return pl.BlockSpec((1, tm, d), lambda b, t: (b, t, 0))


def _prev_rows_spec(tm, d):
    per = tm // SUBLANES
    return pl.BlockSpec((1, SUBLANES, d), lambda b, t: (b, jnp.maximum(t * per - 1, 0), 0))


def _rwkv_pre_kernel(has_vres, *refs):
    if has_vres:
        (x_ref, xp_ref, g_ref, mu_ref, wr_ref, wk_ref, wv_ref, w0_ref, w1_ref, w2_ref,
         a0_ref, a1_ref, a2_ref, g1_ref, g2_ref, kk_ref, ka_ref, rk_ref, e_ref,
         vf_ref, v0_ref, v1_ref, v2_ref,
         r_out, w_out, k_out, v_out, al_out, be_out, g_out, bo_out) = refs
    else:
        (x_ref, xp_ref, g_ref, mu_ref, wr_ref, wk_ref, wv_ref, w0_ref, w1_ref, w2_ref,
         a0_ref, a1_ref, a2_ref, g1_ref, g2_ref, kk_ref, ka_ref, rk_ref, e_ref,
         r_out, w_out, k_out, v_out, al_out, be_out, g_out, bo_out) = refs
    t = pl.program_id(1)
    g = g_ref[...]
    h = _rms(x_ref[0], g)
    hp = _rms(xp_ref[0], g)
    prev = jnp.where(t > 0, hp[SUBLANES - 1:SUBLANES, :], 0.0)
    row = lax.broadcasted_iota(jnp.int32, h.shape, 0)
    hs = jnp.where(row == 0, prev, pltpu.roll(h, 1, axis=0))
    xx = hs - h

    def mix(i):
        return (h + xx * mu_ref[i:i + 1, :]).astype(BF16)

    e = e_ref[...]
    r = _dot(mix(0), wr_ref[...])
    wl = _dot(jnp.tanh(_dot(mix(1), w1_ref[...])).astype(BF16), w2_ref[...])
    k = _dot(mix(2), wk_ref[...])
    xv = mix(3)
    v = _dot(xv, wv_ref[...])
    al = _dot(_dot(mix(4), a1_ref[...]).astype(BF16), a2_ref[...])
    gate = _dot(_sigmoid(_dot(mix(5), g1_ref[...])).astype(BF16), g2_ref[...])

    logw = -_softplus(-(w0_ref[...] + wl)) - 0.5
    log_decay = -jnp.exp(logw)
    a = _sigmoid(a0_ref[...] + al)
    kk = k * kk_ref[...]
    kk = kk / jnp.maximum(jnp.sqrt(_seg_sum(kk * kk, e)), 1e-12)
    k = k * (1.0 + (a - 1.0) * ka_ref[...])
    if has_vres:
        vl = _dot(_dot(xv, v1_ref[...]).astype(BF16), v2_ref[...])
        v = v + (vf_ref[0] - v) * _sigmoid(v0_ref[...] + vl)
    r_out[0] = r
    w_out[0] = log_decay
    k_out[0] = k
    v_out[0] = v
    al_out[0] = -kk
    be_out[0] = kk * a
    g_out[0] = gate
    bo_out[0] = _seg_sum(r * k * rk_ref[...], e) * v


def _rwkv_pre(x, g, mu, wr, wk, wv, w0, w1, w2, a0, a1, a2, g1, g2, k_k, k_a, r_k, v_first, v_res, tm):
    B, T, D = x.shape
    has_vres = v_res is not None
    row = lambda p: p.reshape(1, D)
    bf = lambda p: p.astype(BF16)
    args = [x, x, row(g), mu, bf(wr), bf(wk), bf(wv), row(w0), bf(w1), bf(w2),
            row(a0), bf(a1), bf(a2), bf(g1), bf(g2), row(k_k), row(k_a), row(r_k), _seg_matrix()]
    specs = [_tile_spec(tm, D), _prev_rows_spec(tm, D)] + [_const_spec(a.shape) for a in args[2:]]
    if has_vres:
        v0, v1, v2 = v_res
        extra = [v_first, row(v0), bf(v1), bf(v2)]
        args += extra
        specs += [_tile_spec(tm, D)] + [_const_spec(a.shape) for a in extra[1:]]
    out = jax.ShapeDtypeStruct((B, T, D), F32)
    return pl.pallas_call(
        functools.partial(_rwkv_pre_kernel, has_vres),
        grid=(B, T // tm),
        in_specs=specs,
        out_specs=[_tile_spec(tm, D)] * 8,
        out_shape=[out] * 8,
        compiler_params=_params(2),
        name="rwkv_pre",
    )(*args)


def _dot_nt(a, b):
    return lax.dot_general(a, b, (((1,), (1,)), ((), ())), preferred_element_type=F32)


def _dot_tn(a, b):
    return lax.dot_general(a, b, (((0,), (0,)), ((), ())), preferred_element_type=F32)


def _rwkv_chunk_kernel(r_ref, lw_ref, k_ref, v_ref, al_ref, be_ref, tril_ref, bdm_ref, y_ref, z_ref):
    nb, L, D = r_ref.shape
    reps = GROUP_LANES // L

    @pl.when(pl.program_id(1) == 0)
    def _():
        z_ref[...] = jnp.zeros_like(z_ref)

    tril = tril_ref[...]
    bdm = bdm_ref[...]

    def bd(side):
        return jnp.concatenate([side.astype(BF16)] * reps, axis=0) * bdm

    trow = lax.broadcasted_iota(jnp.int32, (L, GROUP_LANES), 0)
    tcol = lax.broadcasted_iota(jnp.int32, (L, GROUP_LANES), 1) % L
    strict = tcol < trow
    incl = tcol <= trow
    eye = jnp.where(tcol == trow, 1.0, 0.0)
    bdm32 = bdm.astype(F32)
    ng = D // GROUP_LANES
    unit = [(bb, slice(g * GROUP_LANES, (g + 1) * GROUP_LANES)) for bb in range(nb) for g in range(ng)]
    groups = range(len(unit))

    at_all, rt_all, bt_all, kt_all, bh_all, kh_all, p_all = [], [], [], [], [], [], []
    for bb in range(nb):
        lw = lw_ref[bb]
        hi = lw.astype(BF16)
        rest = lw - hi.astype(F32)
        mid = rest.astype(BF16)
        lo = (rest - mid.astype(F32)).astype(BF16)
        cum = _dot(tril, hi) + _dot(tril, mid) + _dot(tril, lo)
        cum_last = cum[L - 1:L, :]
        e_neg = jnp.exp(-cum)
        e_end = jnp.exp(cum_last - cum)
        at_all.append((al_ref[bb] * jnp.exp(cum - lw)).astype(BF16))
        rt_all.append((r_ref[bb] * jnp.exp(cum)).astype(BF16))
        bt_all.append((be_ref[bb] * e_neg).astype(BF16))
        kt_all.append((k_ref[bb] * e_neg).astype(BF16))
        bh_all.append((be_ref[bb] * e_end).astype(BF16))
        kh_all.append((k_ref[bb] * e_end).astype(BF16))
        p_all.append(jnp.exp(cum_last))

    at, rt, bt, kt, bh, kh, p_last = ([a[bb][:, s] for bb, s in unit]
                                      for a in (at_all, rt_all, bt_all, kt_all, bh_all, kh_all, p_all))
    v = [v_ref[bb, :, s] for bb, s in unit]
    gm = [_dot_nt(jnp.concatenate([at[g], rt[g]], axis=0),
                  jnp.concatenate([bd(bt[g]), bd(kt[g])], axis=0)) for g in groups]
    m = [jnp.where(strict, gm[g][:L, :GROUP_LANES], 0.0) for g in groups]
    aak = [jnp.where(strict, gm[g][:L, GROUP_LANES:], 0.0).astype(BF16) for g in groups]
    arb = [jnp.where(incl, gm[g][L:, :GROUP_LANES], 0.0).astype(BF16) for g in groups]
    ark = [jnp.where(incl, gm[g][L:, GROUP_LANES:], 0.0).astype(BF16) for g in groups]
    zt = [z_ref[g] for g in groups]
    ztb = [zt[g].astype(BF16) for g in groups]
    bdv = [bd(v[g]) for g in groups]
    w1 = [_dot_nt(at[g], ztb[g]) + _dot(aak[g], bdv[g]) for g in groups]
    y0 = [_dot_nt(rt[g], ztb[g]) + _dot(ark[g], bdv[g]) for g in groups]
    steps = L.bit_length() - 2
    x = [eye + m[g] for g in groups]
    m = [_dot(m[g].astype(BF16), bd(m[g])) for g in groups]
    for i in range(steps):
        mb = [m[g].astype(BF16) for g in groups]
        if i == steps - 1:
            x = [x[g] + _dot(mb[g], bd(x[g])) for g in groups]
        else:
            x, m = ([x[g] + _dot(mb[g], bd(x[g])) for g in groups],
                    [_dot(mb[g], bd(m[g])) for g in groups])
    xb = [x[g].astype(BF16) for g in groups]
    u = [_dot(xb[g], bd(w1[g])) for g in groups]
    y = [y0[g] + _dot(arb[g], bd(u[g])) for g in groups]
    upd = [_dot_tn(jnp.concatenate([u[g], v[g]], axis=0).astype(BF16),
                   jnp.concatenate([bh[g], kh[g]], axis=0)) for g in groups]
    for g in groups:
        bb, s = unit[g]
        y_ref[bb, :, s] = y[g]
        z_ref[g] = p_last[g] * zt[g] + upd[g] * bdm32


def _rwkv_chunk(r, lw, k, v, al, be, chunk):
    B, T, D = r.shape
    i = lax.broadcasted_iota(jnp.int32, (chunk, chunk), 0)
    j = lax.broadcasted_iota(jnp.int32, (chunk, chunk), 1)
    consts = [(j <= i).astype(BF16), _seg_matrix()[:GROUP_LANES, :GROUP_LANES]]
    nb = REC_BATCH if B % REC_BATCH == 0 else 1
    spec = pl.BlockSpec((nb, chunk, D), lambda b, t: (b, t, 0))
    return pl.pallas_call(
        _rwkv_chunk_kernel,
        grid=(B // nb, T // chunk),
        in_specs=[spec] * 6 + [_const_spec(c.shape) for c in consts],
        out_specs=spec,
        out_shape=jax.ShapeDtypeStruct((B, T, D), F32),
        scratch_shapes=[pltpu.VMEM((nb * (D // GROUP_LANES), GROUP_LANES, GROUP_LANES), F32)],
        compiler_params=_params(2),
        name="rwkv_chunk",
    )(r, lw, k, v, al, be, *consts)


def _rwkv_out(y, bonus, gate, ln_g, ln_b, w_o, e):
    inv_n = 1.0 / RWKV_HEAD
    d = y - _seg_sum(y, e) * inv_n
    var = _seg_sum(d * d, e) * inv_n
    yn = d * lax.rsqrt(var + GN_EPS) * ln_g + ln_b
    return _dot(((yn + bonus) * gate).astype(BF16), w_o)


def _conformer_kernel(kw, tm, x_ref, g_ref, win_ref, bin_ref, dw_ref, dwb_ref, lng_ref, lnb_ref,
                      wout_ref, bout_ref, o_ref, cbuf, obuf):
    D = x_ref.shape[-1]
    hdr = cbuf.shape[0] - tm
    x = x_ref[0]
    h = _rms(x, g_ref[...]).astype(BF16)
    u = _dot(h, win_ref[...]) + bin_ref[...]
    glu = u[:, :D] * _sigmoid(u[:, D:])

    @pl.when(pl.program_id(1) == 0)
    def _():
        cbuf[0:hdr, :] = jnp.zeros((hdr, D), F32)

    cbuf[hdr:hdr + tm, :] = glu
    rb = min(128, tm)
    cb = 128
    base0 = hdr - (kw - 1)
    for c in range(D // cb):
        cs = slice(c * cb, (c + 1) * cb)
        for r0 in range(0, tm, rb):
            acc = None
            for s in range(SUBLANES):
                rows = rb + (SUBLANES if s else 0)
                part = None
                for k in range(kw):
                    if (base0 + k) % SUBLANES != s:
                        continue
                    a0 = r0 + base0 + k - s
                    term = dw_ref[k:k + 1, cs] * cbuf[a0:a0 + rows, cs]
                    part = term if part is None else part + term
                if part is None:
                    continue
                part = part[s:s + rb]
                acc = part if acc is None else acc + part
            obuf[r0:r0 + rb, cs] = acc
    cbuf[0:hdr, :] = cbuf[tm:tm + hdr, :]

    c = obuf[...] + dwb_ref[...]
    mu = jnp.mean(c, axis=-1, keepdims=True)
    var = jnp.mean(jnp.square(c - mu), axis=-1, keepdims=True)
    z = (c - mu) * lax.rsqrt(var + LN_EPS) * lng_ref[...] + lnb_ref[...]
    z = (z * _sigmoid(z)).astype(BF16)
    o_ref[0] = x + (_dot(z, wout_ref[...]) + bout_ref[...])


def _conformer(x, g, w_in, b_in, dw, dw_b, ln_g, ln_b, w_out, b_out, tm):
    B, T, D = x.shape
    kw = dw.shape[0]
    hdr = -(-(kw - 1) // SUBLANES) * SUBLANES
    row = lambda p: p.reshape(1, -1)
    consts = [row(g), w_in.astype(BF16), row(b_in), dw, row(dw_b), row(ln_g), row(ln_b),
              w_out.astype(BF16), row(b_out)]
    return pl.pallas_call(
        functools.partial(_conformer_kernel, kw, tm),
        grid=(B, T // tm),
        in_specs=[_tile_spec(tm, D)] + [_const_spec(c.shape) for c in consts],
        out_specs=_tile_spec(tm, D),
        out_shape=jax.ShapeDtypeStruct((B, T, D), F32),
        scratch_shapes=[pltpu.VMEM((hdr + tm, D), F32), pltpu.VMEM((tm, D), F32)],
        compiler_params=_params(2),
        name="conformer",
    )(x, *consts)


def _kv_kernel(mem_ref, g_ref, w_ref, o_ref):
    memn = _rms(mem_ref[0], g_ref[...]).astype(BF16)
    o_ref[0, 0] = _dot(memn, w_ref[0]).astype(BF16)


def _kv_proj(mem, g, w_kv):
    B, M, D = mem.shape
    L = w_kv.shape[0]
    return pl.pallas_call(
        _kv_kernel,
        grid=(L, B),
        in_specs=[pl.BlockSpec((1, M, D), lambda l, b: (b, 0, 0)),
                  pl.BlockSpec((1, D), lambda l, b: (0, 0)),
                  pl.BlockSpec((1, D, 2 * D), lambda l, b: (l, 0, 0))],
        out_specs=pl.BlockSpec((1, 1, M, 2 * D), lambda l, b: (l, b, 0, 0)),
        out_shape=jax.ShapeDtypeStruct((L, B, M, 2 * D), BF16),
        compiler_params=_params(2),
        name="kv_proj",
    )(mem, g.reshape(1, D), w_kv.astype(BF16))


def _xattn_kernel(has_mix, *refs):
    if has_mix:
        (x_ref, y_ref, bo_ref, gt_ref, lng_ref, lnb_ref, wom_ref, e_ref,
         g_ref, wq_ref, kv_ref, wo_ref, o_ref) = refs
        x = x_ref[0] + _rwkv_out(y_ref[0], bo_ref[0], gt_ref[0], lng_ref[...], lnb_ref[...],
                                 wom_ref[...], e_ref[...])
    else:
        x_ref, g_ref, wq_ref, kv_ref, wo_ref, o_ref = refs
        x = x_ref[0]
    D = x_ref.shape[-1]
    hd = D // XATTN_HEADS
    h = _rms(x, g_ref[...]).astype(BF16)
    q = _dot(h, wq_ref[...]).astype(BF16)
    heads = range(XATTN_HEADS)
    s = [_dot_nt(q[:, i * hd:(i + 1) * hd], kv_ref[0, 0, :, i * hd:(i + 1) * hd]) * (hd ** -0.5)
         for i in heads]
    e = [jnp.exp(s[i] - jnp.max(s[i], axis=-1, keepdims=True)) for i in heads]
    p = [(e[i] * (1.0 / jnp.sum(e[i], axis=-1, keepdims=True))).astype(BF16) for i in heads]
    outs = [_dot(p[i], kv_ref[0, 0, :, D + i * hd:D + (i + 1) * hd]).astype(BF16) for i in heads]
    o = jnp.concatenate(outs, axis=-1)
    o_ref[0] = x + _dot(o, wo_ref[...])


def _xattn(x, g, w_q, kv, layer, w_o, tm, mix=None):
    B, T, D = x.shape
    M = kv.shape[2]
    row = lambda p: p.reshape(1, D)
    tiles = [x]
    consts = []
    if mix is not None:
        y, bonus, gate, ln_g, ln_b, w_o_mix = mix
        tiles += [y, bonus, gate]
        consts += [row(ln_g), row(ln_b), w_o_mix.astype(BF16), _seg_matrix()]
    consts += [row(g), w_q.astype(BF16)]
    return pl.pallas_call(
        functools.partial(_xattn_kernel, mix is not None),
        grid=(B, T // tm),
        in_specs=[_tile_spec(tm, D)] * len(tiles) + [_const_spec(c.shape) for c in consts]
        + [pl.BlockSpec((1, 1, M, 2 * D), lambda b, t: (layer, b, 0, 0)), _const_spec((D, D))],
        out_specs=_tile_spec(tm, D),
        out_shape=jax.ShapeDtypeStruct((B, T, D), F32),
        compiler_params=_params(2),
        name="xattn",
    )(*tiles, *consts, kv, w_o.astype(BF16))


def _ffn_kernel(kw, cw, rb, has_final, *refs):
    if has_final:
        (x_ref, g_ref, win_ref, dw_ref, wout_ref, fg_ref, o_ref,
         carry, h_ref, ug_ref, uv_ref, act_ref) = refs
    else:
        (x_ref, g_ref, win_ref, dw_ref, wout_ref, o_ref,
         carry, h_ref, ug_ref, uv_ref, act_ref) = refs
    tm = x_ref.shape[1]
    F = wout_ref.shape[0]
    hdr = SUBLANES
    h_ref[...] = _rms(x_ref[0], g_ref[...]).astype(BF16)

    @pl.when(pl.program_id(1) == 0)
    def _():
        carry[...] = jnp.zeros_like(carry)

    def proj(slot, c0):
        for u_ref, cc in ((ug_ref, c0), (uv_ref, F + c0)):
            u_ref[slot, 0:hdr, :] = carry[:, cc:cc + cw]
            u_ref[slot, hdr:hdr + tm, :] = _dot(h_ref[...], win_ref[:, cc:cc + cw])
            carry[:, cc:cc + cw] = u_ref[slot, tm:tm + hdr, :]

    def conv_rows(u_ref, slot, r0, cc):
        blk = u_ref[slot, r0:r0 + rb + hdr, :]
        out = dw_ref[kw - 1:kw, cc:cc + cw] * blk[hdr:, :]
        for k in range(kw - 1):
            off = hdr - (kw - 1) + k
            out = out + dw_ref[k:k + 1, cc:cc + cw] * blk[off:off + rb, :]
        return out

    starts = list(range(0, F, cw))
    proj(0, starts[0])
    for i, c0 in enumerate(starts):
        slot = i % 2
        if i + 1 < len(starts):
            proj(1 - slot, starts[i + 1])
        for r0 in range(0, tm, rb):
            gate = conv_rows(ug_ref, slot, r0, c0)
            val = conv_rows(uv_ref, slot, r0, F + c0)
            act_ref[r0:r0 + rb, c0:c0 + cw] = (gate * _sigmoid(gate) * val).astype(BF16)
    y = x_ref[0] + _dot(act_ref[...], wout_ref[...])
    if has_final:
        y = _rms(y, fg_ref[...])
    o_ref[0] = y


def _ffn(x, g, w_in, dw, w_out, final_g, tm, cw):
    B, T, D = x.shape
    kw = dw.shape[0]
    F = w_out.shape[0]
    has_final = final_g is not None
    consts = [g.reshape(1, D), w_in.astype(BF16), dw, w_out.astype(BF16)]
    if has_final:
        consts.append(final_g.reshape(1, D))
    return pl.pallas_call(
        functools.partial(_ffn_kernel, kw, cw, FFN_RB, has_final),
        grid=(B, T // tm),
        in_specs=[_tile_spec(tm, D)] + [_const_spec(c.shape) for c in consts],
        out_specs=_tile_spec(tm, D),
        out_shape=jax.ShapeDtypeStruct((B, T, D), F32),
        scratch_shapes=[pltpu.VMEM((SUBLANES, 2 * F), F32),
                        pltpu.VMEM((tm, D), BF16),
                        pltpu.VMEM((2, SUBLANES + tm, cw), F32),
                        pltpu.VMEM((2, SUBLANES + tm, cw), F32),
                        pltpu.VMEM((tm, F), BF16)],
        compiler_params=_params(2),
        name="ffn",
    )(x, *consts)


TM = 256
TM_WIDE = 512
TM_FFN = 512
REC_CHUNK = 64
REC_BATCH = 2
FFN_CW = 256
FFN_RB = 64


def rwkv_layer(x, g, mu, w_r, w_k, w_v, w_o, w0, w1, w2, a0, a1, a2, g1, g2, k_k, k_a, r_k,
               ln_g, ln_b, v_first, v_res, tm, tb):
    r, w, k, v, al, be, gate, bonus = _rwkv_pre(x, g, mu, w_r, w_k, w_v, w0, w1, w2, a0, a1, a2, g1, g2,
                                                k_k, k_a, r_k.reshape(-1), v_first, v_res, tm)
    y = _rwkv_chunk(r, w, k, v, al, be, tb)
    return (y, bonus, gate, ln_g, ln_b, w_o), v


def kernel(x, mem, mem_norm_g, norm_mix_g, norm_xattn_g, norm_ffn_g, final_norm_g, rwkv_mu, rwkv_w_r, rwkv_w_k, rwkv_w_v, rwkv_w_o, rwkv_w0, rwkv_w1, rwkv_w2, rwkv_a0, rwkv_a1, rwkv_a2, rwkv_g1, rwkv_g2, rwkv_k_k, rwkv_k_a, rwkv_r_k, rwkv_ln_g, rwkv_ln_b, rwkv_v0, rwkv_v1, rwkv_v2, conv_w_in, conv_b_in, conv_dw, conv_dw_b, conv_ln_g, conv_ln_b, conv_w_out, conv_b_out, xattn_w_q, xattn_w_kv, xattn_w_o, ffn_w_in, ffn_dw, ffn_w_out):
    B, T, D = x.shape
    depth = norm_mix_g.shape[0]
    tm = min(TM, T)
    tmw = min(TM_WIDE, T)
    kv = _kv_proj(mem, mem_norm_g, xattn_w_kv)
    v_first = None
    ia = 0
    ib = 0
    for layer in range(depth):
        if layer % 2 == 0:
            v_res = None if ia == 0 else (rwkv_v0[ia - 1], rwkv_v1[ia - 1], rwkv_v2[ia - 1])
            mix, v = rwkv_layer(
                x, norm_mix_g[layer], rwkv_mu[ia], rwkv_w_r[ia], rwkv_w_k[ia], rwkv_w_v[ia],
                rwkv_w_o[ia], rwkv_w0[ia], rwkv_w1[ia], rwkv_w2[ia], rwkv_a0[ia], rwkv_a1[ia],
                rwkv_a2[ia], rwkv_g1[ia], rwkv_g2[ia], rwkv_k_k[ia], rwkv_k_a[ia], rwkv_r_k[ia],
                rwkv_ln_g[ia], rwkv_ln_b[ia], v_first, v_res, tm, min(REC_CHUNK, T))
            if ia == 0:
                v_first = v
            ia += 1
        else:
            mix = None
            x = _conformer(x, norm_mix_g[layer], conv_w_in[ib], conv_b_in[ib], conv_dw[ib],
                           conv_dw_b[ib], conv_ln_g[ib], conv_ln_b[ib], conv_w_out[ib],
                           conv_b_out[ib], tm)
            ib += 1
        x = _xattn(x, norm_xattn_g[layer], xattn_w_q[layer], kv, layer, xattn_w_o[layer], tmw, mix)
        x = _ffn(x, norm_ffn_g[layer], ffn_w_in[layer], ffn_dw[layer], ffn_w_out[layer],
                 final_norm_g if layer == depth - 1 else None, min(TM_FFN, T), FFN_CW)
    return x
```

```python
import functools

import jax
import jax.numpy as jnp
from jax import lax
from jax.experimental import pallas as pl
from jax.experimental.pallas import tpu as pltpu

F32 = jnp.float32
BF16 = jnp.bfloat16

RWKV_HEAD = 64
GN_EPS = 64e-5
NORM_EPS = 1e-6
LN_EPS = 1e-5
XATTN_HEADS = 4
SUBLANES = 8
SEG_BLOCK = 256
GROUP_LANES = 128
VMEM_LIMIT = 56 * 1024 * 1024


def _params(n_grid):
    return pltpu.CompilerParams(
        dimension_semantics=("arbitrary",) * n_grid, vmem_limit_bytes=VMEM_LIMIT)


def _dot(a, b):
    return jnp.dot(a, b, preferred_element_type=F32)


def _rms(x, g):
    return x * lax.rsqrt(jnp.mean(x * x, axis=-1, keepdims=True) + NORM_EPS) * g


def _sigmoid(x):
    return 1.0 / (1.0 + jnp.exp(-x))


def _softplus(x):
    return jnp.maximum(x, 0.0) + jnp.log(1.0 + jnp.exp(-jnp.abs(x)))


def _seg_sum(y, e):
    outs = []
    for c in range(y.shape[-1] // SEG_BLOCK):
        yb = y[:, c * SEG_BLOCK:(c + 1) * SEG_BLOCK]
        hi = yb.astype(BF16)
        lo = (yb - hi.astype(F32)).astype(BF16)
        outs.append(_dot(hi, e) + _dot(lo, e))
    return jnp.concatenate(outs, axis=-1)


def _seg_matrix():
    i = lax.broadcasted_iota(jnp.int32, (SEG_BLOCK, SEG_BLOCK), 0) // RWKV_HEAD
    j = lax.broadcasted_iota(jnp.int32, (SEG_BLOCK, SEG_BLOCK), 1) // RWKV_HEAD
    return (i == j).astype(BF16)


def _const_spec(shape):
    zeros = (0,) * len(shape)
    return pl.BlockSpec(shape, lambda *_: zeros, pipeline_mode=pl.Buffered(1))


def _tile_spec(tm, d):
    return pl.BlockSpec((1, tm, d), lambda b, t: (b, t, 0))


def _prev_rows_spec(tm, d):
    per = tm // SUBLANES
    return pl.BlockSpec((1, SUBLANES, d), lambda b, t: (b, jnp.maximum(t * per - 1, 0), 0))


def _rwkv_pre_kernel(has_vres, *refs):
    if has_vres:
        (x_ref, xp_ref, g_ref, mu_ref, wr_ref, wk_ref, wv_ref, w0_ref, w1_ref, w2_ref,
         a0_ref, a1_ref, a2_ref, g1_ref, g2_ref, kk_ref, ka_ref, rk_ref, e_ref,
         vf_ref, v0_ref, v1_ref, v2_ref,
         r_out, w_out, k_out, v_out, al_out, be_out, g_out, bo_out) = refs
    else:
        (x_ref, xp_ref, g_ref, mu_ref, wr_ref, wk_ref, wv_ref, w0_ref, w1_ref, w2_ref,
         a0_ref, a1_ref, a2_ref, g1_ref, g2_ref, kk_ref, ka_ref, rk_ref, e_ref,
         r_out, w_out, k_out, v_out, al_out, be_out, g_out, bo_out) = refs
    t = pl.program_id(1)
    g = g_ref[...]
    h = _rms(x_ref[0], g)
    hp = _rms(xp_ref[0], g)
    prev = jnp.where(t > 0, hp[SUBLANES - 1:SUBLANES, :], 0.0)
    row = lax.broadcasted_iota(jnp.int32, h.shape, 0)
    hs = jnp.where(row == 0, prev, pltpu.roll(h, 1, axis=0))
    xx = hs - h

    def mix(i):
        return (h + xx * mu_ref[i:i + 1, :]).astype(BF16)

    e = e_ref[...]
    r = _dot(mix(0), wr_ref[...])
    wl = _dot(jnp.tanh(_dot(mix(1), w1_ref[...])).astype(BF16), w2_ref[...])
    k = _dot(mix(2), wk_ref[...])
    xv = mix(3)
    v = _dot(xv, wv_ref[...])
    al = _dot(_dot(mix(4), a1_ref[...]).astype(BF16), a2_ref[...])
    gate = _dot(_sigmoid(_dot(mix(5), g1_ref[...])).astype(BF16), g2_ref[...])

    logw = -_softplus(-(w0_ref[...] + wl)) - 0.5
    log_decay = -jnp.exp(logw)
    a = _sigmoid(a0_ref[...] + al)
    kk = k * kk_ref[...]
    kk = kk / jnp.maximum(jnp.sqrt(_seg_sum(kk * kk, e)), 1e-12)
    k = k * (1.0 + (a - 1.0) * ka_ref[...])
    if has_vres:
        vl = _dot(_dot(xv, v1_ref[...]).astype(BF16), v2_ref[...])
        v = v + (vf_ref[0] - v) * _sigmoid(v0_ref[...] + vl)
    r_out[0] = r
    w_out[0] = log_decay
    k_out[0] = k
    v_out[0] = v
    al_out[0] = -kk
    be_out[0] = kk * a
    g_out[0] = gate
    bo_out[0] = _seg_sum(r * k * rk_ref[...], e) * v


def _rwkv_pre(x, g, mu, wr, wk, wv, w0, w1, w2, a0, a1, a2, g1, g2, k_k, k_a, r_k, v_first, v_res, tm):
    B, T, D = x.shape
    has_vres = v_res is not None
    row = lambda p: p.reshape(1, D)
    bf = lambda p: p.astype(BF16)
    args = [x, x, row(g), mu, bf(wr), bf(wk), bf(wv), row(w0), bf(w1), bf(w2),
            row(a0), bf(a1), bf(a2), bf(g1), bf(g2), row(k_k), row(k_a), row(r_k), _seg_matrix()]
    specs = [_tile_spec(tm, D), _prev_rows_spec(tm, D)] + [_const_spec(a.shape) for a in args[2:]]
    if has_vres:
        v0, v1, v2 = v_res
        extra = [v_first, row(v0), bf(v1), bf(v2)]
        args += extra
        specs += [_tile_spec(tm, D)] + [_const_spec(a.shape) for a in extra[1:]]
    out = jax.ShapeDtypeStruct((B, T, D), F32)
    return pl.pallas_call(
        functools.partial(_rwkv_pre_kernel, has_vres),
        grid=(B, T // tm),
        in_specs=specs,
        out_specs=[_tile_spec(tm, D)] * 8,
        out_shape=[out] * 8,
        compiler_params=_params(2),
        name="rwkv_pre",
    )(*args)


def _dot_nt(a, b):
    return lax.dot_general(a, b, (((1,), (1,)), ((), ())), preferred_element_type=F32)


def _dot_tn(a, b):
    return lax.dot_general(a, b, (((0,), (0,)), ((), ())), preferred_element_type=F32)


def _rwkv_chunk_kernel(r_ref, lw_ref, k_ref, v_ref, al_ref, be_ref, tril_ref, bdm_ref, y_ref, z_ref):
    nb, L, D = r_ref.shape
    reps = GROUP_LANES // L

    @pl.when(pl.program_id(1) == 0)
    def _():
        z_ref[...] = jnp.zeros_like(z_ref)

    tril = tril_ref[...]
    bdm = bdm_ref[...]

    def bd(side):
        return jnp.concatenate([side.astype(BF16)] * reps, axis=0) * bdm

    trow = lax.broadcasted_iota(jnp.int32, (L, GROUP_LANES), 0)
    tcol = lax.broadcasted_iota(jnp.int32, (L, GROUP_LANES), 1) % L
    strict = tcol < trow
    incl = tcol <= trow
    eye = jnp.where(tcol == trow, 1.0, 0.0)
    bdm32 = bdm.astype(F32)
    ng = D // GROUP_LANES
    unit = [(bb, slice(g * GROUP_LANES, (g + 1) * GROUP_LANES)) for bb in range(nb) for g in range(ng)]
    groups = range(len(unit))

    at_all, rt_all, bt_all, kt_all, bh_all, kh_all, p_all = [], [], [], [], [], [], []
    for bb in range(nb):
        lw = lw_ref[bb]
        hi = lw.astype(BF16)
        rest = lw - hi.astype(F32)
        mid = rest.astype(BF16)
        lo = (rest - mid.astype(F32)).astype(BF16)
        cum = _dot(tril, hi) + _dot(tril, mid) + _dot(tril, lo)
        cum_last = cum[L - 1:L, :]
        e_neg = jnp.exp(-cum)
        e_end = jnp.exp(cum_last - cum)
        at_all.append((al_ref[bb] * jnp.exp(cum - lw)).astype(BF16))
        rt_all.append((r_ref[bb] * jnp.exp(cum)).astype(BF16))
        bt_all.append((be_ref[bb] * e_neg).astype(BF16))
        kt_all.append((k_ref[bb] * e_neg).astype(BF16))
        bh_all.append((be_ref[bb] * e_end).astype(BF16))
        kh_all.append((k_ref[bb] * e_end).astype(BF16))
        p_all.append(jnp.exp(cum_last))

    at, rt, bt, kt, bh, kh, p_last = ([a[bb][:, s] for bb, s in unit]
                                      for a in (at_all, rt_all, bt_all, kt_all, bh_all, kh_all, p_all))
    v = [v_ref[bb, :, s] for bb, s in unit]
    ar = [jnp.concatenate([at[g], rt[g]], axis=0) for g in groups]
    gm = [_dot_nt(ar[g], jnp.concatenate([bd(bt[g]), bd(kt[g])], axis=0)) for g in groups]
    m = [jnp.where(strict, gm[g][:L, :GROUP_LANES], 0.0) for g in groups]
    arb = [jnp.where(incl, gm[g][L:, :GROUP_LANES], 0.0).astype(BF16) for g in groups]
    both_k = [jnp.concatenate([jnp.where(strict, gm[g][:L, GROUP_LANES:], 0.0),
                               jnp.where(incl, gm[g][L:, GROUP_LANES:], 0.0)], axis=0).astype(BF16)
              for g in groups]
    zt = [z_ref[g] for g in groups]
    ztb = [zt[g].astype(BF16) for g in groups]
    bdv = [bd(v[g]) for g in groups]
    wy = [_dot_nt(ar[g], ztb[g]) + _dot(both_k[g], bdv[g]) for g in groups]
    w1 = [wy[g][:L] for g in groups]
    y0 = [wy[g][L:] for g in groups]
    steps = L.bit_length() - 2
    x = [eye + m[g] for g in groups]
    m = [_dot(m[g].astype(BF16), bd(m[g])) for g in groups]
    for i in range(steps):
        mb = [m[g].astype(BF16) for g in groups]
        if i == steps - 1:
            x = [x[g] + _dot(mb[g], bd(x[g])) for g in groups]
        else:
            both = [_dot(mb[g], jnp.concatenate([bd(x[g]), bd(m[g])], axis=1)) for g in groups]
            x = [x[g] + both[g][:, :GROUP_LANES] for g in groups]
            m = [both[g][:, GROUP_LANES:] for g in groups]
    xb = [x[g].astype(BF16) for g in groups]
    u = [_dot(xb[g], bd(w1[g])) for g in groups]
    y = [y0[g] + _dot(arb[g], bd(u[g])) for g in groups]
    upd = [_dot_tn(jnp.concatenate([u[g], v[g]], axis=0).astype(BF16),
                   jnp.concatenate([bh[g], kh[g]], axis=0)) for g in groups]
    for g in groups:
        bb, s = unit[g]
        y_ref[bb, :, s] = y[g]
        z_ref[g] = p_last[g] * zt[g] + upd[g] * bdm32


def _rwkv_chunk(r, lw, k, v, al, be, chunk):
    B, T, D = r.shape
    i = lax.broadcasted_iota(jnp.int32, (chunk, chunk), 0)
    j = lax.broadcasted_iota(jnp.int32, (chunk, chunk), 1)
    consts = [(j <= i).astype(BF16), _seg_matrix()[:GROUP_LANES, :GROUP_LANES]]
    nb = REC_BATCH if B % REC_BATCH == 0 else 1
    spec = pl.BlockSpec((nb, chunk, D), lambda b, t: (b, t, 0))
    return pl.pallas_call(
        _rwkv_chunk_kernel,
        grid=(B // nb, T // chunk),
        in_specs=[spec] * 6 + [_const_spec(c.shape) for c in consts],
        out_specs=spec,
        out_shape=jax.ShapeDtypeStruct((B, T, D), F32),
        scratch_shapes=[pltpu.VMEM((nb * (D // GROUP_LANES), GROUP_LANES, GROUP_LANES), F32)],
        compiler_params=_params(2),
        name="rwkv_chunk",
    )(r, lw, k, v, al, be, *consts)


def _rwkv_out(y, bonus, gate, ln_g, ln_b, w_o, e):
    inv_n = 1.0 / RWKV_HEAD
    d = y - _seg_sum(y, e) * inv_n
    var = _seg_sum(d * d, e) * inv_n
    yn = d * lax.rsqrt(var + GN_EPS) * ln_g + ln_b
    return _dot(((yn + bonus) * gate).astype(BF16), w_o)


def _conformer_kernel(kw, tm, x_ref, g_ref, win_ref, bin_ref, dw_ref, dwb_ref, lng_ref, lnb_ref,
                      wout_ref, bout_ref, o_ref, cbuf, obuf):
    D = x_ref.shape[-1]
    hdr = cbuf.shape[0] - tm
    x = x_ref[0]
    h = _rms(x, g_ref[...]).astype(BF16)
    u = _dot(h, win_ref[...]) + bin_ref[...]
    glu = u[:, :D] * _sigmoid(u[:, D:])

    @pl.when(pl.program_id(1) == 0)
    def _():
        cbuf[0:hdr, :] = jnp.zeros((hdr, D), F32)

    cbuf[hdr:hdr + tm, :] = glu
    rb = min(128, tm)
    cb = 128
    base0 = hdr - (kw - 1)
    for c in range(D // cb):
        cs = slice(c * cb, (c + 1) * cb)
        for r0 in range(0, tm, rb):
            acc = None
            for s in range(SUBLANES):
                rows = rb + (SUBLANES if s else 0)
                part = None
                for k in range(kw):
                    if (base0 + k) % SUBLANES != s:
                        continue
                    a0 = r0 + base0 + k - s
                    term = dw_ref[k:k + 1, cs] * cbuf[a0:a0 + rows, cs]
                    part = term if part is None else part + term
                if part is None:
                    continue
                part = part[s:s + rb]
                acc = part if acc is None else acc + part
            obuf[r0:r0 + rb, cs] = acc
    cbuf[0:hdr, :] = cbuf[tm:tm + hdr, :]

    c = obuf[...] + dwb_ref[...]
    mu = jnp.mean(c, axis=-1, keepdims=True)
    var = jnp.mean(jnp.square(c - mu), axis=-1, keepdims=True)
    z = (c - mu) * lax.rsqrt(var + LN_EPS) * lng_ref[...] + lnb_ref[...]
    z = (z * _sigmoid(z)).astype(BF16)
    o_ref[0] = x + (_dot(z, wout_ref[...]) + bout_ref[...])


def _conformer(x, g, w_in, b_in, dw, dw_b, ln_g, ln_b, w_out, b_out, tm):
    B, T, D = x.shape
    kw = dw.shape[0]
    hdr = -(-(kw - 1) // SUBLANES) * SUBLANES
    row = lambda p: p.reshape(1, -1)
    consts = [row(g), w_in.astype(BF16), row(b_in), dw, row(dw_b), row(ln_g), row(ln_b),
              w_out.astype(BF16), row(b_out)]
    return pl.pallas_call(
        functools.partial(_conformer_kernel, kw, tm),
        grid=(B, T // tm),
        in_specs=[_tile_spec(tm, D)] + [_const_spec(c.shape) for c in consts],
        out_specs=_tile_spec(tm, D),
        out_shape=jax.ShapeDtypeStruct((B, T, D), F32),
        scratch_shapes=[pltpu.VMEM((hdr + tm, D), F32), pltpu.VMEM((tm, D), F32)],
        compiler_params=_params(2),
        name="conformer",
    )(x, *consts)


def _kv_kernel(mem_ref, g_ref, w_ref, o_ref):
    memn = _rms(mem_ref[0], g_ref[...]).astype(BF16)
    o_ref[0, 0] = _dot(memn, w_ref[0]).astype(BF16)


def _kv_proj(mem, g, w_kv):
    B, M, D = mem.shape
    L = w_kv.shape[0]
    return pl.pallas_call(
        _kv_kernel,
        grid=(L, B),
        in_specs=[pl.BlockSpec((1, M, D), lambda l, b: (b, 0, 0)),
                  pl.BlockSpec((1, D), lambda l, b: (0, 0)),
                  pl.BlockSpec((1, D, 2 * D), lambda l, b: (l, 0, 0))],
        out_specs=pl.BlockSpec((1, 1, M, 2 * D), lambda l, b: (l, b, 0, 0)),
        out_shape=jax.ShapeDtypeStruct((L, B, M, 2 * D), BF16),
        compiler_params=_params(2),
        name="kv_proj",
    )(mem, g.reshape(1, D), w_kv.astype(BF16))


def _xattn_kernel(has_mix, *refs):
    if has_mix:
        (x_ref, y_ref, bo_ref, gt_ref, lng_ref, lnb_ref, wom_ref, e_ref,
         g_ref, wq_ref, kv_ref, wo_ref, o_ref) = refs
        x = x_ref[0] + _rwkv_out(y_ref[0], bo_ref[0], gt_ref[0], lng_ref[...], lnb_ref[...],
                                 wom_ref[...], e_ref[...])
    else:
        x_ref, g_ref, wq_ref, kv_ref, wo_ref, o_ref = refs
        x = x_ref[0]
    D = x_ref.shape[-1]
    hd = D // XATTN_HEADS
    h = _rms(x, g_ref[...]).astype(BF16)
    q = _dot(h, wq_ref[...]).astype(BF16)
    heads = range(XATTN_HEADS)
    s = [_dot_nt(q[:, i * hd:(i + 1) * hd], kv_ref[0, 0, :, i * hd:(i + 1) * hd]) * (hd ** -0.5)
         for i in heads]
    e = [jnp.exp(s[i] - jnp.max(s[i], axis=-1, keepdims=True)) for i in heads]
    p = [(e[i] * (1.0 / jnp.sum(e[i], axis=-1, keepdims=True))).astype(BF16) for i in heads]
    outs = [_dot(p[i], kv_ref[0, 0, :, D + i * hd:D + (i + 1) * hd]).astype(BF16) for i in heads]
    o = jnp.concatenate(outs, axis=-1)
    o_ref[0] = x + _dot(o, wo_ref[...])


def _xattn(x, g, w_q, kv, layer, w_o, tm, mix=None):
    B, T, D = x.shape
    M = kv.shape[2]
    row = lambda p: p.reshape(1, D)
    tiles = [x]
    consts = []
    if mix is not None:
        y, bonus, gate, ln_g, ln_b, w_o_mix = mix
        tiles += [y, bonus, gate]
        consts += [row(ln_g), row(ln_b), w_o_mix.astype(BF16), _seg_matrix()]
    consts += [row(g), w_q.astype(BF16)]
    return pl.pallas_call(
        functools.partial(_xattn_kernel, mix is not None),
        grid=(B, T // tm),
        in_specs=[_tile_spec(tm, D)] * len(tiles) + [_const_spec(c.shape) for c in consts]
        + [pl.BlockSpec((1, 1, M, 2 * D), lambda b, t: (layer, b, 0, 0)), _const_spec((D, D))],
        out_specs=_tile_spec(tm, D),
        out_shape=jax.ShapeDtypeStruct((B, T, D), F32),
        compiler_params=_params(2),
        name="xattn",
    )(*tiles, *consts, kv, w_o.astype(BF16))


def _ffn_kernel(kw, cw, rb, has_final, *refs):
    if has_final:
        (x_ref, g_ref, win_ref, dw_ref, wout_ref, fg_ref, o_ref,
         carry, h_ref, ug_ref, uv_ref, act_ref) = refs
    else:
        (x_ref, g_ref, win_ref, dw_ref, wout_ref, o_ref,
         carry, h_ref, ug_ref, uv_ref, act_ref) = refs
    tm = x_ref.shape[1]
    F = wout_ref.shape[0]
    hdr = SUBLANES
    h_ref[...] = _rms(x_ref[0], g_ref[...]).astype(BF16)

    @pl.when(pl.program_id(1) == 0)
    def _():
        carry[...] = jnp.zeros_like(carry)

    def proj(slot, c0):
        for u_ref, cc in ((ug_ref, c0), (uv_ref, F + c0)):
            u_ref[slot, 0:hdr, :] = carry[:, cc:cc + cw]
            u_ref[slot, hdr:hdr + tm, :] = _dot(h_ref[...], win_ref[:, cc:cc + cw])
            carry[:, cc:cc + cw] = u_ref[slot, tm:tm + hdr, :]

    def conv_rows(u_ref, slot, r0, cc):
        blk = u_ref[slot, r0:r0 + rb + hdr, :]
        out = dw_ref[kw - 1:kw, cc:cc + cw] * blk[hdr:, :]
        for k in range(kw - 1):
            off = hdr - (kw - 1) + k
            out = out + dw_ref[k:k + 1, cc:cc + cw] * blk[off:off + rb, :]
        return out

    starts = list(range(0, F, cw))
    proj(0, starts[0])
    for i, c0 in enumerate(starts):
        slot = i % 2
        if i + 1 < len(starts):
            proj(1 - slot, starts[i + 1])
        for r0 in range(0, tm, rb):
            gate = conv_rows(ug_ref, slot, r0, c0)
            val = conv_rows(uv_ref, slot, r0, F + c0)
            act_ref[r0:r0 + rb, c0:c0 + cw] = (gate * _sigmoid(gate) * val).astype(BF16)
    y = x_ref[0] + _dot(act_ref[...], wout_ref[...])
    if has_final:
        y = _rms(y, fg_ref[...])
    o_ref[0] = y


def _ffn(x, g, w_in, dw, w_out, final_g, tm, cw):
    B, T, D = x.shape
    kw = dw.shape[0]
    F = w_out.shape[0]
    has_final = final_g is not None
    consts = [g.reshape(1, D), w_in.astype(BF16), dw, w_out.astype(BF16)]
    if has_final:
        consts.append(final_g.reshape(1, D))
    return pl.pallas_call(
        functools.partial(_ffn_kernel, kw, cw, FFN_RB, has_final),
        grid=(B, T // tm),
        in_specs=[_tile_spec(tm, D)] + [_const_spec(c.shape) for c in consts],
        out_specs=_tile_spec(tm, D),
        out_shape=jax.ShapeDtypeStruct((B, T, D), F32),
        scratch_shapes=[pltpu.VMEM((SUBLANES, 2 * F), F32),
                        pltpu.VMEM((tm, D), BF16),
                        pltpu.VMEM((2, SUBLANES + tm, cw), F32),
                        pltpu.VMEM((2, SUBLANES + tm, cw), F32),
                        pltpu.VMEM((tm, F), BF16)],
        compiler_params=_params(2),
        name="ffn",
    )(x, *consts)


TM = 256
TM_WIDE = 512
TM_FFN = 512
REC_CHUNK = 64
REC_BATCH = 2
FFN_CW = 256
FFN_RB = 64


def rwkv_layer(x, g, mu, w_r, w_k, w_v, w_o, w0, w1, w2, a0, a1, a2, g1, g2, k_k, k_a, r_k,
               ln_g, ln_b, v_first, v_res, tm, tb):
    r, w, k, v, al, be, gate, bonus = _rwkv_pre(x, g, mu, w_r, w_k, w_v, w0, w1, w2, a0, a1, a2, g1, g2,
                                                k_k, k_a, r_k.reshape(-1), v_first, v_res, tm)
    y = _rwkv_chunk(r, w, k, v, al, be, tb)
    return (y, bonus, gate, ln_g, ln_b, w_o), v


def kernel(x, mem, mem_norm_g, norm_mix_g, norm_xattn_g, norm_ffn_g, final_norm_g, rwkv_mu, rwkv_w_r, rwkv_w_k, rwkv_w_v, rwkv_w_o, rwkv_w0, rwkv_w1, rwkv_w2, rwkv_a0, rwkv_a1, rwkv_a2, rwkv_g1, rwkv_g2, rwkv_k_k, rwkv_k_a, rwkv_r_k, rwkv_ln_g, rwkv_ln_b, rwkv_v0, rwkv_v1, rwkv_v2, conv_w_in, conv_b_in, conv_dw, conv_dw_b, conv_ln_g, conv_ln_b, conv_w_out, conv_b_out, xattn_w_q, xattn_w_kv, xattn_w_o, ffn_w_in, ffn_dw, ffn_w_out):
    B, T, D = x.shape
    depth = norm_mix_g.shape[0]
    tm = min(TM, T)
    tmw = min(TM_WIDE, T)
    kv = _kv_proj(mem, mem_norm_g, xattn_w_kv)
    v_first = None
    ia = 0
    ib = 0
    for layer in range(depth):
        if layer % 2 == 0:
            v_res = None if ia == 0 else (rwkv_v0[ia - 1], rwkv_v1[ia - 1], rwkv_v2[ia - 1])
            mix, v = rwkv_layer(
                x, norm_mix_g[layer], rwkv_mu[ia], rwkv_w_r[ia], rwkv_w_k[ia], rwkv_w_v[ia],
                rwkv_w_o[ia], rwkv_w0[ia], rwkv_w1[ia], rwkv_w2[ia], rwkv_a0[ia], rwkv_a1[ia],
                rwkv_a2[ia], rwkv_g1[ia], rwkv_g2[ia], rwkv_k_k[ia], rwkv_k_a[ia], rwkv_r_k[ia],
                rwkv_ln_g[ia], rwkv_ln_b[ia], v_first, v_res, tm, min(REC_CHUNK, T))
            if ia == 0:
                v_first = v
            ia += 1
        else:
            mix = None
            x = _conformer(x, norm_mix_g[layer], conv_w_in[ib], conv_b_in[ib], conv_dw[ib],
                           conv_dw_b[ib], conv_ln_g[ib], conv_ln_b[ib], conv_w_out[ib],
                           conv_b_out[ib], tm)
            ib += 1
        x = _xattn(x, norm_xattn_g[layer], xattn_w_q[layer], kv, layer, xattn_w_o[layer], tmw, mix)
        x = _ffn(x, norm_ffn_g[layer], ffn_w_in[layer], ffn_dw[layer], ffn_w_out[layer],
                 final_norm_g if layer == depth - 1 else None, min(TM_FFN, T), FFN_CW)
    return x
```

```python
import functools

import jax
import jax.numpy as jnp
from jax import lax
from jax.experimental import pallas as pl
from jax.experimental.pallas import tpu as pltpu

F32 = jnp.float32
BF16 = jnp.bfloat16

RWKV_HEAD = 64
GN_EPS = 64e-5
NORM_EPS = 1e-6
LN_EPS = 1e-5
XATTN_HEADS = 4
SUBLANES = 8
LANES = 128
CONV_RB = 128
SEG_BLOCK = 256
GROUP_LANES = 128
VMEM_LIMIT = 56 * 1024 * 1024


def _params(n_grid):
    return pltpu.CompilerParams(
        dimension_semantics=("arbitrary",) * n_grid, vmem_limit_bytes=VMEM_LIMIT)


def _dot(a, b):
    return jnp.dot(a, b, preferred_element_type=F32)


def _rms(x, g):
    return x * lax.rsqrt(jnp.mean(x * x, axis=-1, keepdims=True) + NORM_EPS) * g


def _sigmoid(x):
    return 1.0 / (1.0 + jnp.exp(-x))


def _softplus(x):
    return jnp.maximum(x, 0.0) + jnp.log(1.0 + jnp.exp(-jnp.abs(x)))


def _seg_sum(y, e):
    outs = []
    for c in range(y.shape[-1] // SEG_BLOCK):
        yb = y[:, c * SEG_BLOCK:(c + 1) * SEG_BLOCK]
        hi = yb.astype(BF16)
        lo = (yb - hi.astype(F32)).astype(BF16)
        outs.append(_dot(hi, e) + _dot(lo, e))
    return jnp.concatenate(outs, axis=-1)


def _seg_matrix():
    i = lax.broadcasted_iota(jnp.int32, (SEG_BLOCK, SEG_BLOCK), 0) // RWKV_HEAD
    j = lax.broadcasted_iota(jnp.int32, (SEG_BLOCK, SEG_BLOCK), 1) // RWKV_HEAD
    return (i == j).astype(BF16)


def _const_spec(shape):
    zeros = (0,) * len(shape)
    return pl.BlockSpec(shape, lambda *_: zeros, pipeline_mode=pl.Buffered(1))


def _tile_spec(tm, d):
    return pl.BlockSpec((1, tm, d), lambda b, t: (b, t, 0))


def _prev_rows_spec(tm, d):
    per = tm // SUBLANES
    return pl.BlockSpec((1, SUBLANES, d), lambda b, t: (b, jnp.maximum(t * per - 1, 0), 0))


def _rwkv_pre_kernel(has_vres, *refs):
    if has_vres:
        (x_ref, xp_ref, g_ref, mu_ref, wr_ref, wk_ref, wv_ref, w0_ref, w1_ref, w2_ref,
         a0_ref, a1_ref, a2_ref, g1_ref, g2_ref, kk_ref, ka_ref, rk_ref, e_ref,
         vf_ref, v0_ref, v1_ref, v2_ref,
         r_out, w_out, k_out, v_out, al_out, be_out, g_out, bo_out) = refs
    else:
        (x_ref, xp_ref, g_ref, mu_ref, wr_ref, wk_ref, wv_ref, w0_ref, w1_ref, w2_ref,
         a0_ref, a1_ref, a2_ref, g1_ref, g2_ref, kk_ref, ka_ref, rk_ref, e_ref,
         r_out, w_out, k_out, v_out, al_out, be_out, g_out, bo_out) = refs
    t = pl.program_id(1)
    g = g_ref[...]
    h = _rms(x_ref[0], g)
    hp = _rms(xp_ref[0], g)
    prev = jnp.where(t > 0, hp[SUBLANES - 1:SUBLANES, :], 0.0)
    row = lax.broadcasted_iota(jnp.int32, h.shape, 0)
    hs = jnp.where(row == 0, prev, pltpu.roll(h, 1, axis=0))
    xx = hs - h

    def mix(i):
        return (h + xx * mu_ref[i:i + 1, :]).astype(BF16)

    e = e_ref[...]
    r = _dot(mix(0), wr_ref[...])
    wl = _dot(jnp.tanh(_dot(mix(1), w1_ref[...])).astype(BF16), w2_ref[...])
    k = _dot(mix(2), wk_ref[...])
    xv = mix(3)
    v = _dot(xv, wv_ref[...])
    al = _dot(_dot(mix(4), a1_ref[...]).astype(BF16), a2_ref[...])
    gate = _dot(_sigmoid(_dot(mix(5), g1_ref[...])).astype(BF16), g2_ref[...])

    logw = -_softplus(-(w0_ref[...] + wl)) - 0.5
    log_decay = -jnp.exp(logw)
    a = _sigmoid(a0_ref[...] + al)
    kk = k * kk_ref[...]
    kk = kk / jnp.maximum(jnp.sqrt(_seg_sum(kk * kk, e)), 1e-12)
    k = k * (1.0 + (a - 1.0) * ka_ref[...])
    if has_vres:
        vl = _dot(_dot(xv, v1_ref[...]).astype(BF16), v2_ref[...])
        v = v + (vf_ref[0] - v) * _sigmoid(v0_ref[...] + vl)
    r_out[0] = r
    w_out[0] = log_decay
    k_out[0] = k
    v_out[0] = v
    al_out[0] = -kk
    be_out[0] = kk * a
    g_out[0] = gate
    bo_out[0] = _seg_sum(r * k * rk_ref[...], e) * v


def _rwkv_pre(x, g, mu, wr, wk, wv, w0, w1, w2, a0, a1, a2, g1, g2, k_k, k_a, r_k, v_first, v_res, tm):
    B, T, D = x.shape
    has_vres = v_res is not None
    row = lambda p: p.reshape(1, D)
    bf = lambda p: p.astype(BF16)
    args = [x, x, row(g), mu, bf(wr), bf(wk), bf(wv), row(w0), bf(w1), bf(w2),
            row(a0), bf(a1), bf(a2), bf(g1), bf(g2), row(k_k), row(k_a), row(r_k), _seg_matrix()]
    specs = [_tile_spec(tm, D), _prev_rows_spec(tm, D)] + [_const_spec(a.shape) for a in args[2:]]
    if has_vres:
        v0, v1, v2 = v_res
        extra = [v_first, row(v0), bf(v1), bf(v2)]
        args += extra
        specs += [_tile_spec(tm, D)] + [_const_spec(a.shape) for a in extra[1:]]
    out = jax.ShapeDtypeStruct((B, T, D), F32)
    return pl.pallas_call(
        functools.partial(_rwkv_pre_kernel, has_vres),
        grid=(B, T // tm),
        in_specs=specs,
        out_specs=[_tile_spec(tm, D)] * 8,
        out_shape=[out] * 8,
        compiler_params=_params(2),
        name="rwkv_pre",
    )(*args)


def _dot_nt(a, b):
    return lax.dot_general(a, b, (((1,), (1,)), ((), ())), preferred_element_type=F32)


def _dot_tn(a, b):
    return lax.dot_general(a, b, (((0,), (0,)), ((), ())), preferred_element_type=F32)


def _rwkv_chunk_kernel(r_ref, lw_ref, k_ref, v_ref, al_ref, be_ref, tril_ref, bdm_ref, y_ref, z_ref):
    nb, L, D = r_ref.shape
    reps = GROUP_LANES // L

    @pl.when(pl.program_id(1) == 0)
    def _():
        z_ref[...] = jnp.zeros_like(z_ref)

    tril = tril_ref[...]
    bdm = bdm_ref[...]

    def bd(side):
        return jnp.concatenate([side.astype(BF16)] * reps, axis=0) * bdm

    trow = lax.broadcasted_iota(jnp.int32, (L, GROUP_LANES), 0)
    tcol = lax.broadcasted_iota(jnp.int32, (L, GROUP_LANES), 1) % L
    strict = tcol < trow
    incl = tcol <= trow
    eye = jnp.where(tcol == trow, 1.0, 0.0)
    bdm32 = bdm.astype(F32)
    ng = D // GROUP_LANES
    unit = [(bb, slice(g * GROUP_LANES, (g + 1) * GROUP_LANES)) for bb in range(nb) for g in range(ng)]
    groups = range(len(unit))

    at_all, rt_all, bt_all, kt_all, bh_all, kh_all, p_all = [], [], [], [], [], [], []
    for bb in range(nb):
        lw = lw_ref[bb]
        hi = lw.astype(BF16)
        rest = lw - hi.astype(F32)
        mid = rest.astype(BF16)
        lo = (rest - mid.astype(F32)).astype(BF16)
        cum = _dot(tril, hi) + _dot(tril, mid) + _dot(tril, lo)
        cum_last = cum[L - 1:L, :]
        e_neg = jnp.exp(-cum)
        e_end = jnp.exp(cum_last - cum)
        at_all.append((al_ref[bb] * jnp.exp(cum - lw)).astype(BF16))
        rt_all.append((r_ref[bb] * jnp.exp(cum)).astype(BF16))
        bt_all.append((be_ref[bb] * e_neg).astype(BF16))
        kt_all.append((k_ref[bb] * e_neg).astype(BF16))
        bh_all.append((be_ref[bb] * e_end).astype(BF16))
        kh_all.append((k_ref[bb] * e_end).astype(BF16))
        p_all.append(jnp.exp(cum_last))

    at, rt, bt, kt, bh, kh, p_last = ([a[bb][:, s] for bb, s in unit]
                                      for a in (at_all, rt_all, bt_all, kt_all, bh_all, kh_all, p_all))
    v = [v_ref[bb, :, s] for bb, s in unit]
    ar = [jnp.concatenate([at[g], rt[g]], axis=0) for g in groups]
    gm = [_dot_nt(ar[g], jnp.concatenate([bd(bt[g]), bd(kt[g])], axis=0)) for g in groups]
    m = [jnp.where(strict, gm[g][:L, :GROUP_LANES], 0.0) for g in groups]
    arb = [jnp.where(incl, gm[g][L:, :GROUP_LANES], 0.0).astype(BF16) for g in groups]
    both_k = [jnp.concatenate([jnp.where(strict, gm[g][:L, GROUP_LANES:], 0.0),
                               jnp.where(incl, gm[g][L:, GROUP_LANES:], 0.0)], axis=0).astype(BF16)
              for g in groups]
    zt = [z_ref[g] for g in groups]
    ztb = [zt[g].astype(BF16) for g in groups]
    bdv = [bd(v[g]) for g in groups]
    wy = [_dot_nt(ar[g], ztb[g]) + _dot(both_k[g], bdv[g]) for g in groups]
    w1 = [wy[g][:L] for g in groups]
    y0 = [wy[g][L:] for g in groups]
    steps = L.bit_length() - 2
    x = [eye + m[g] for g in groups]
    m = [_dot(m[g].astype(BF16), bd(m[g])) for g in groups]
    for i in range(steps):
        mb = [m[g].astype(BF16) for g in groups]
        if i == steps - 1:
            x = [x[g] + _dot(mb[g], bd(x[g])) for g in groups]
        else:
            both = [_dot(mb[g], jnp.concatenate([bd(x[g]), bd(m[g])], axis=1)) for g in groups]
            x = [x[g] + both[g][:, :GROUP_LANES] for g in groups]
            m = [both[g][:, GROUP_LANES:] for g in groups]
    xb = [x[g].astype(BF16) for g in groups]
    u = [_dot(xb[g], bd(w1[g])) for g in groups]
    y = [y0[g] + _dot(arb[g], bd(u[g])) for g in groups]
    upd = [_dot_tn(jnp.concatenate([u[g], v[g]], axis=0).astype(BF16),
                   jnp.concatenate([bh[g], kh[g]], axis=0)) for g in groups]
    for g in groups:
        bb, s = unit[g]
        y_ref[bb, :, s] = y[g]
        z_ref[g] = p_last[g] * zt[g] + upd[g] * bdm32


def _rwkv_chunk(r, lw, k, v, al, be, chunk):
    B, T, D = r.shape
    i = lax.broadcasted_iota(jnp.int32, (chunk, chunk), 0)
    j = lax.broadcasted_iota(jnp.int32, (chunk, chunk), 1)
    consts = [(j <= i).astype(BF16), _seg_matrix()[:GROUP_LANES, :GROUP_LANES]]
    nb = REC_BATCH if B % REC_BATCH == 0 else 1
    spec = pl.BlockSpec((nb, chunk, D), lambda b, t: (b, t, 0))
    return pl.pallas_call(
        _rwkv_chunk_kernel,
        grid=(B // nb, T // chunk),
        in_specs=[spec] * 6 + [_const_spec(c.shape) for c in consts],
        out_specs=spec,
        out_shape=jax.ShapeDtypeStruct((B, T, D), F32),
        scratch_shapes=[pltpu.VMEM((nb * (D // GROUP_LANES), GROUP_LANES, GROUP_LANES), F32)],
        compiler_params=_params(2),
        name="rwkv_chunk",
    )(r, lw, k, v, al, be, *consts)


def _rwkv_out(y, bonus, gate, ln_g, ln_b, w_o, e):
    inv_n = 1.0 / RWKV_HEAD
    d = y - _seg_sum(y, e) * inv_n
    var = _seg_sum(d * d, e) * inv_n
    yn = d * lax.rsqrt(var + GN_EPS) * ln_g + ln_b
    return _dot(((yn + bonus) * gate).astype(BF16), w_o)


def _conformer_kernel(kw, tm, x_ref, g_ref, win_ref, bin_ref, dw_ref, dwb_ref, lng_ref, lnb_ref,
                      wout_ref, bout_ref, o_ref, cbuf, obuf):
    D = x_ref.shape[-1]
    hdr = cbuf.shape[0] - tm
    x = x_ref[0]
    h = _rms(x, g_ref[...]).astype(BF16)
    u = _dot(h, win_ref[...]) + bin_ref[...]
    glu = u[:, :D] * _sigmoid(u[:, D:])

    @pl.when(pl.program_id(1) == 0)
    def _():
        cbuf[0:hdr, :] = jnp.zeros((hdr, D), F32)

    cbuf[hdr:hdr + tm, :] = glu
    rb = min(CONV_RB, tm)
    cb = LANES
    base0 = hdr - (kw - 1)
    for c in range(D // cb):
        cs = slice(c * cb, (c + 1) * cb)
        for r0 in range(0, tm, rb):
            acc = None
            for s in range(SUBLANES):
                rows = rb + (SUBLANES if s else 0)
                part = None
                for k in range(kw):
                    if (base0 + k) % SUBLANES != s:
                        continue
                    a0 = r0 + base0 + k - s
                    term = dw_ref[k:k + 1, cs] * cbuf[a0:a0 + rows, cs]
                    part = term if part is None else part + term
                if part is None:
                    continue
                part = part[s:s + rb]
                acc = part if acc is None else acc + part
            obuf[r0:r0 + rb, cs] = acc
    cbuf[0:hdr, :] = cbuf[tm:tm + hdr, :]

    c = obuf[...] + dwb_ref[...]
    mu = jnp.mean(c, axis=-1, keepdims=True)
    var = jnp.mean(jnp.square(c - mu), axis=-1, keepdims=True)
    z = (c - mu) * lax.rsqrt(var + LN_EPS) * lng_ref[...] + lnb_ref[...]
    z = (z * _sigmoid(z)).astype(BF16)
    o_ref[0] = x + (_dot(z, wout_ref[...]) + bout_ref[...])


def _conformer(x, g, w_in, b_in, dw, dw_b, ln_g, ln_b, w_out, b_out, tm):
    B, T, D = x.shape
    kw = dw.shape[0]
    hdr = -(-(kw - 1) // SUBLANES) * SUBLANES
    row = lambda p: p.reshape(1, -1)
    consts = [row(g), w_in.astype(BF16), row(b_in), dw, row(dw_b), row(ln_g), row(ln_b),
              w_out.astype(BF16), row(b_out)]
    return pl.pallas_call(
        functools.partial(_conformer_kernel, kw, tm),
        grid=(B, T // tm),
        in_specs=[_tile_spec(tm, D)] + [_const_spec(c.shape) for c in consts],
        out_specs=_tile_spec(tm, D),
        out_shape=jax.ShapeDtypeStruct((B, T, D), F32),
        scratch_shapes=[pltpu.VMEM((hdr + tm, D), F32), pltpu.VMEM((tm, D), F32)],
        compiler_params=_params(2),
        name="conformer",
    )(x, *consts)


def _kv_kernel(mem_ref, g_ref, w_ref, o_ref):
    memn = _rms(mem_ref[0], g_ref[...]).astype(BF16)
    o_ref[0, 0] = _dot(memn, w_ref[0]).astype(BF16)


def _kv_proj(mem, g, w_kv):
    B, M, D = mem.shape
    L = w_kv.shape[0]
    return pl.pallas_call(
        _kv_kernel,
        grid=(L, B),
        in_specs=[pl.BlockSpec((1, M, D), lambda l, b: (b, 0, 0)),
                  pl.BlockSpec((1, D), lambda l, b: (0, 0)),
                  pl.BlockSpec((1, D, 2 * D), lambda l, b: (l, 0, 0))],
        out_specs=pl.BlockSpec((1, 1, M, 2 * D), lambda l, b: (l, b, 0, 0)),
        out_shape=jax.ShapeDtypeStruct((L, B, M, 2 * D), BF16),
        compiler_params=_params(2),
        name="kv_proj",
    )(mem, g.reshape(1, D), w_kv.astype(BF16))


def _xattn_kernel(has_mix, *refs):
    if has_mix:
        (x_ref, y_ref, bo_ref, gt_ref, lng_ref, lnb_ref, wom_ref, e_ref,
         g_ref, wq_ref, kv_ref, wo_ref, o_ref) = refs
        x = x_ref[0] + _rwkv_out(y_ref[0], bo_ref[0], gt_ref[0], lng_ref[...], lnb_ref[...],
                                 wom_ref[...], e_ref[...])
    else:
        x_ref, g_ref, wq_ref, kv_ref, wo_ref, o_ref = refs
        x = x_ref[0]
    D = x_ref.shape[-1]
    hd = D // XATTN_HEADS
    h = _rms(x, g_ref[...]).astype(BF16)
    q = _dot(h, wq_ref[...]).astype(BF16)
    heads = range(XATTN_HEADS)
    s = [_dot_nt(q[:, i * hd:(i + 1) * hd], kv_ref[0, 0, :, i * hd:(i + 1) * hd]) * (hd ** -0.5)
         for i in heads]
    e = [jnp.exp(s[i] - jnp.max(s[i], axis=-1, keepdims=True)) for i in heads]
    p = [(e[i] * (1.0 / jnp.sum(e[i], axis=-1, keepdims=True))).astype(BF16) for i in heads]
    outs = [_dot(p[i], kv_ref[0, 0, :, D + i * hd:D + (i + 1) * hd]).astype(BF16) for i in heads]
    o = jnp.concatenate(outs, axis=-1)
    o_ref[0] = x + _dot(o, wo_ref[...])


def _xattn(x, g, w_q, kv, layer, w_o, tm, mix=None):
    B, T, D = x.shape
    M = kv.shape[2]
    row = lambda p: p.reshape(1, D)
    tiles = [x]
    consts = []
    if mix is not None:
        y, bonus, gate, ln_g, ln_b, w_o_mix = mix
        tiles += [y, bonus, gate]
        consts += [row(ln_g), row(ln_b), w_o_mix.astype(BF16), _seg_matrix()]
    consts += [row(g), w_q.astype(BF16)]
    return pl.pallas_call(
        functools.partial(_xattn_kernel, mix is not None),
        grid=(B, T // tm),
        in_specs=[_tile_spec(tm, D)] * len(tiles) + [_const_spec(c.shape) for c in consts]
        + [pl.BlockSpec((1, 1, M, 2 * D), lambda b, t: (layer, b, 0, 0)), _const_spec((D, D))],
        out_specs=_tile_spec(tm, D),
        out_shape=jax.ShapeDtypeStruct((B, T, D), F32),
        compiler_params=_params(2),
        name="xattn",
    )(*tiles, *consts, kv, w_o.astype(BF16))


def _ffn_kernel(kw, cw, rb, has_final, *refs):
    if has_final:
        (x_ref, g_ref, win_ref, dw_ref, wout_ref, fg_ref, o_ref,
         carry, h_ref, ug_ref, uv_ref, act_ref) = refs
    else:
        (x_ref, g_ref, win_ref, dw_ref, wout_ref, o_ref,
         carry, h_ref, ug_ref, uv_ref, act_ref) = refs
    tm = x_ref.shape[1]
    F = wout_ref.shape[0]
    hdr = SUBLANES
    h_ref[...] = _rms(x_ref[0], g_ref[...]).astype(BF16)

    @pl.when(pl.program_id(1) == 0)
    def _():
        carry[...] = jnp.zeros_like(carry)

    def proj(slot, c0):
        for u_ref, cc in ((ug_ref, c0), (uv_ref, F + c0)):
            u_ref[slot, 0:hdr, :] = carry[:, cc:cc + cw]
            u_ref[slot, hdr:hdr + tm, :] = _dot(h_ref[...], win_ref[:, cc:cc + cw])
            carry[:, cc:cc + cw] = u_ref[slot, tm:tm + hdr, :]

    def conv_rows(u_ref, slot, r0, cc):
        blk = u_ref[slot, r0:r0 + rb + hdr, :]
        out = dw_ref[kw - 1:kw, cc:cc + cw] * blk[hdr:, :]
        for k in range(kw - 1):
            off = hdr - (kw - 1) + k
            out = out + dw_ref[k:k + 1, cc:cc + cw] * blk[off:off + rb, :]
        return out

    starts = list(range(0, F, cw))
    proj(0, starts[0])
    for i, c0 in enumerate(starts):
        slot = i % 2
        if i + 1 < len(starts):
            proj(1 - slot, starts[i + 1])
        for r0 in range(0, tm, rb):
            gate = conv_rows(ug_ref, slot, r0, c0)
            val = conv_rows(uv_ref, slot, r0, F + c0)
            act_ref[r0:r0 + rb, c0:c0 + cw] = (gate * _sigmoid(gate) * val).astype(BF16)
    y = x_ref[0] + _dot(act_ref[...], wout_ref[...])
    if has_final:
        y = _rms(y, fg_ref[...])
    o_ref[0] = y


def _ffn(x, g, w_in, dw, w_out, final_g, tm, cw):
    B, T, D = x.shape
    kw = dw.shape[0]
    F = w_out.shape[0]
    has_final = final_g is not None
    consts = [g.reshape(1, D), w_in.astype(BF16), dw, w_out.astype(BF16)]
    if has_final:
        consts.append(final_g.reshape(1, D))
    return pl.pallas_call(
        functools.partial(_ffn_kernel, kw, cw, FFN_RB, has_final),
        grid=(B, T // tm),
        in_specs=[_tile_spec(tm, D)] + [_const_spec(c.shape) for c in consts],
        out_specs=_tile_spec(tm, D),
        out_shape=jax.ShapeDtypeStruct((B, T, D), F32),
        scratch_shapes=[pltpu.VMEM((SUBLANES, 2 * F), F32),
                        pltpu.VMEM((tm, D), BF16),
                        pltpu.VMEM((2, SUBLANES + tm, cw), F32),
                        pltpu.VMEM((2, SUBLANES + tm, cw), F32),
                        pltpu.VMEM((tm, F), BF16)],
        compiler_params=_params(2),
        name="ffn",
    )(x, *consts)


TM = 256
TM_WIDE = 512
TM_FFN = 512
REC_CHUNK = 64
REC_BATCH = 4
FFN_CW = 256
FFN_RB = 64


def rwkv_layer(x, g, mu, w_r, w_k, w_v, w_o, w0, w1, w2, a0, a1, a2, g1, g2, k_k, k_a, r_k,
               ln_g, ln_b, v_first, v_res, tm, tb):
    r, w, k, v, al, be, gate, bonus = _rwkv_pre(x, g, mu, w_r, w_k, w_v, w0, w1, w2, a0, a1, a2, g1, g2,
                                                k_k, k_a, r_k.reshape(-1), v_first, v_res, tm)
    y = _rwkv_chunk(r, w, k, v, al, be, tb)
    return (y, bonus, gate, ln_g, ln_b, w_o), v


def kernel(x, mem, mem_norm_g, norm_mix_g, norm_xattn_g, norm_ffn_g, final_norm_g, rwkv_mu, rwkv_w_r, rwkv_w_k, rwkv_w_v, rwkv_w_o, rwkv_w0, rwkv_w1, rwkv_w2, rwkv_a0, rwkv_a1, rwkv_a2, rwkv_g1, rwkv_g2, rwkv_k_k, rwkv_k_a, rwkv_r_k, rwkv_ln_g, rwkv_ln_b, rwkv_v0, rwkv_v1, rwkv_v2, conv_w_in, conv_b_in, conv_dw, conv_dw_b, conv_ln_g, conv_ln_b, conv_w_out, conv_b_out, xattn_w_q, xattn_w_kv, xattn_w_o, ffn_w_in, ffn_dw, ffn_w_out):
    B, T, D = x.shape
    depth = norm_mix_g.shape[0]
    tm = min(TM, T)
    tmw = min(TM_WIDE, T)
    kv = _kv_proj(mem, mem_norm_g, xattn_w_kv)
    v_first = None
    ia = 0
    ib = 0
    for layer in range(depth):
        if layer % 2 == 0:
            v_res = None if ia == 0 else (rwkv_v0[ia - 1], rwkv_v1[ia - 1], rwkv_v2[ia - 1])
            mix, v = rwkv_layer(
                x, norm_mix_g[layer], rwkv_mu[ia], rwkv_w_r[ia], rwkv_w_k[ia], rwkv_w_v[ia],
                rwkv_w_o[ia], rwkv_w0[ia], rwkv_w1[ia], rwkv_w2[ia], rwkv_a0[ia], rwkv_a1[ia],
                rwkv_a2[ia], rwkv_g1[ia], rwkv_g2[ia], rwkv_k_k[ia], rwkv_k_a[ia], rwkv_r_k[ia],
                rwkv_ln_g[ia], rwkv_ln_b[ia], v_first, v_res, tm, min(REC_CHUNK, T))
            if ia == 0:
                v_first = v
            ia += 1
        else:
            mix = None
            x = _conformer(x, norm_mix_g[layer], conv_w_in[ib], conv_b_in[ib], conv_dw[ib],
                           conv_dw_b[ib], conv_ln_g[ib], conv_ln_b[ib], conv_w_out[ib],
                           conv_b_out[ib], tm)
            ib += 1
        x = _xattn(x, norm_xattn_g[layer], xattn_w_q[layer], kv, layer, xattn_w_o[layer], tmw, mix)
        x = _ffn(x, norm_ffn_g[layer], ffn_w_in[layer], ffn_dw[layer], ffn_w_out[layer],
                 final_norm_g if layer == depth - 1 else None, min(TM_FFN, T), FFN_CW)
    return x
```
